```python
import jax, jax.numpy as jnp
from jax import lax
import numpy as np

D_MODEL = 1024
BATCH = 8
SEQ = 2048
DEPTH = 4
DEC_BATCH = 128
DEC_SEQ = 4
PAST_LEN = 16384
PAGE_SIZE = 128

MIX_DIM = D_MODEL
RWKV_DIM = MIX_DIM // 2
HEAD_DIM = 64
RWKV_HEADS = RWKV_DIM // HEAD_DIM
CONV_DIM = MIX_DIM - RWKV_DIM
CONV_WIDTH = 31
DECAY_LORA = 64
AAA_LORA = 64
GATE_LORA = 128
VRES_LORA = 32
SHIFT_DIM = 3 * RWKV_DIM + DECAY_LORA + AAA_LORA + GATE_LORA
PROJ_DIM = SHIFT_DIM + 2 * CONV_DIM
N_EXPERTS = 32
TOP_K = 4
D_FF = D_MODEL
SWIGLU_LIMIT = 7.0
SWIGLU_ALPHA = 1.702
EXPERT_BLOCK = 128
LN_EPS = 1e-5
GN_EPS = 64e-5
DEEPNORM_ALPHA = (2 * DEPTH) ** 0.25
DEEPNORM_BETA = (8 * DEPTH) ** -0.25

kernel_name = "rwkv7_conformer_conv_moe_deepnorm_adaln_step"


def _layernorm(x, g, b, eps=LN_EPS):
    xf = x.astype(jnp.float32)
    mu = xf.mean(-1, keepdims=True)
    var = jnp.square(xf - mu).mean(-1, keepdims=True)
    return ((xf - mu) * lax.rsqrt(var + eps)).astype(x.dtype) * g + b


def _heads(t):
    return t.reshape(t.shape[:-1] + (RWKV_HEADS, HEAD_DIM))


def _wkv_scan(S0, r, w, k, v, a, b):
    def step(S, inp):
        r_t, w_t, k_t, v_t, a_t, b_t = inp
        sa = jnp.einsum('bhij,bhj->bhi', S, a_t)
        S = S * w_t[:, :, None, :] + sa[..., None] * b_t[:, :, None, :] + v_t[..., None] * k_t[:, :, None, :]
        return S, jnp.einsum('bhij,bhj->bhi', S, r_t)
    xs = tuple(jnp.moveaxis(t.astype(jnp.float32), 1, 0) for t in (r, w, k, v, a, b))
    S, ys = lax.scan(step, S0.astype(jnp.float32), xs)
    return jnp.moveaxis(ys, 0, 1), S


def _mixer(h, l, v_first, S0, shift0, conv0, p):
    B, T, _ = h.shape
    z = h @ p['w_in'][l]
    zs, zc = z[..., :SHIFT_DIM], z[..., SHIFT_DIM:]
    zs_prev = jnp.concatenate([shift0[:, None, :].astype(zs.dtype), zs[:, :-1]], axis=1)
    zm = zs + (zs_prev - zs) * p['tm_mu'][l]
    new_shift = zs[:, -1]
    i1, i2, i3 = RWKV_DIM, 2 * RWKV_DIM, 3 * RWKV_DIM
    i4 = i3 + DECAY_LORA
    i5 = i4 + AAA_LORA
    r, k, v = zm[..., :i1], zm[..., i1:i2], zm[..., i2:i3]
    zw, za, zg = zm[..., i3:i4], zm[..., i4:i5], zm[..., i5:]
    w_log = -jax.nn.softplus(-(p['w0'][l] + jnp.tanh(zw) @ p['w2'][l])) - 0.5
    decay = jnp.exp(-jnp.exp(w_log.astype(jnp.float32)))
    a = jax.nn.sigmoid(p['a0'][l] + za @ p['a2'][l])
    g = jax.nn.sigmoid(zg) @ p['g2'][l]
    if l == 0:
        v_first = v
    else:
        vg = jax.nn.sigmoid(p['v0'][l - 1] + (v @ p['v1'][l - 1]) @ p['v2'][l - 1])
        v = v + (v_first - v) * vg
    kk = _heads((k * p['k_k'][l]).astype(jnp.float32))
    kk = kk * lax.rsqrt(jnp.maximum(jnp.sum(kk * kk, -1, keepdims=True), 1e-24))
    k = k * (1 + (a - 1) * p['k_a'][l])
    rh, kh, vh, ah = _heads(r), _heads(k), _heads(v), _heads(a)
    y, S = _wkv_scan(S0, rh, _heads(decay), kh, vh, -kk, kk * ah.astype(jnp.float32))
    ym = y.mean(-1, keepdims=True)
    yv = jnp.square(y - ym).mean(-1, keepdims=True)
    y = ((y - ym) * lax.rsqrt(yv + GN_EPS)).astype(h.dtype)
    y = y * _heads(p['lnx_g'][l]) + _heads(p['lnx_b'][l])
    y = y + jnp.sum(rh * kh * p['r_k'][l], -1, keepdims=True) * vh
    y = y.reshape(B, T, RWKV_DIM) * g
    u = zc[..., :CONV_DIM] * jax.nn.sigmoid(zc[..., CONV_DIM:])
    u_ext = jnp.concatenate([conv0.astype(u.dtype), u], axis=1)
    cv = lax.conv_general_dilated(u_ext, p['conv_w'][l][:, None, :], window_strides=(1,), padding='VALID',
                                  dimension_numbers=('NWC', 'WIO', 'NWC'),
                                  feature_group_count=CONV_DIM) + p['conv_b'][l]
    new_conv = u_ext[:, -(CONV_WIDTH - 1):]
    cv = jax.nn.silu(_layernorm(cv, p['conv_ln_g'][l], p['conv_ln_b'][l]))
    out = jnp.concatenate([y, cv], axis=-1) @ p['w_out'][l]
    return out, v_first, S.astype(h.dtype), new_shift, new_conv


def _moe(h, router_w, router_b, w_gu, b_gu, w_dn, b_dn):
    T, D = h.shape
    TK = T * TOP_K
    logits = (h @ router_w + router_b).astype(jnp.float32)
    top_v, top_i = lax.top_k(logits, TOP_K)
    gates = jax.nn.softmax(top_v, axis=-1).reshape(TK)
    flat_e = top_i.reshape(TK).astype(jnp.int32)
    flat_tok = jnp.arange(TK, dtype=jnp.int32) // TOP_K
    order = jnp.argsort(flat_e)
    e_sorted, tok_sorted, gate_sorted = flat_e[order], flat_tok[order], gates[order]
    counts = jnp.bincount(flat_e, length=N_EXPERTS).astype(jnp.int32)
    padded = (counts + EXPERT_BLOCK - 1) // EXPERT_BLOCK * EXPERT_BLOCK
    padded_end = jnp.cumsum(padded)
    padded_start = padded_end - padded
    start = jnp.cumsum(counts) - counts
    dest = padded_start[e_sorted] + jnp.arange(TK, dtype=jnp.int32) - start[e_sorted]
    n_blocks = -(-TK // EXPERT_BLOCK) + N_EXPERTS
    n_rows = n_blocks * EXPERT_BLOCK
    row_tok = jnp.full((n_rows,), T, jnp.int32).at[dest].set(tok_sorted)
    h_pad = jnp.concatenate([h, jnp.zeros((1, D), h.dtype)], axis=0)
    xb = h_pad[row_tok].reshape(n_blocks, EXPERT_BLOCK, D)
    block_start = jnp.arange(n_blocks, dtype=jnp.int32) * EXPERT_BLOCK
    block_e = jnp.minimum(jnp.searchsorted(padded_end, block_start, side='right'), N_EXPERTS - 1)

    def expert_block(args):
        xblk, e = args
        gu = xblk @ w_gu[e] + b_gu[e]
        gate = jnp.minimum(gu[..., :D_FF], SWIGLU_LIMIT)
        lin = jnp.clip(gu[..., D_FF:], -SWIGLU_LIMIT, SWIGLU_LIMIT)
        act = gate * jax.nn.sigmoid(SWIGLU_ALPHA * gate) * (lin + 1)
        return act @ w_dn[e] + b_dn[e]

    yb = lax.map(expert_block, (xb, block_e)).reshape(n_rows, D)
    y_sorted = yb[dest] * gate_sorted[:, None].astype(h.dtype)
    return jax.ops.segment_sum(y_sorted, tok_sorted, num_segments=T)


def _trunk(x, c, wkv0, shift0, conv0, p):
    B, T, D = x.shape
    v_first = None
    new_wkv, new_shift, new_conv = [], [], []
    cs = jax.nn.silu(c)
    for l in range(DEPTH):
        mod = (cs @ p['ada_w'][l] + p['ada_b'][l])[:, None, :]
        sh1, sc1, g1, sh2, sc2, g2 = jnp.split(mod, 6, axis=-1)
        h = x * (1 + sc1) + sh1
        mix, v_first, S, shf, cnv = _mixer(h, l, v_first, wkv0[l], shift0[l], conv0[l], p)
        x = _layernorm(DEEPNORM_ALPHA * x + g1 * mix, p['ln1_g'][l], p['ln1_b'][l])
        h = x * (1 + sc2) + sh2
        ff = _moe(h.reshape(B * T, D), p['router_w'][l], p['router_b'][l], p['w_gu'][l], p['b_gu'][l],
                  p['w_dn'][l], p['b_dn'][l]).reshape(B, T, D)
        x = _layernorm(DEEPNORM_ALPHA * x + g2 * ff, p['ln2_g'][l], p['ln2_b'][l])
        new_wkv.append(S)
        new_shift.append(shf)
        new_conv.append(cnv)
    return x, jnp.stack(new_wkv), jnp.stack(new_shift), jnp.stack(new_conv)


def setup_inputs(seed: int = 0) -> dict:
    key = jax.random.key(seed)
    ks = jax.random.split(key, 40)
    f32 = jnp.float32
    nrm = lambda k, shape, s: jax.random.normal(k, shape, f32) * s
    L, E = DEPTH, N_EXPERTS
    return {
        'x_prompt': nrm(ks[0], (BATCH, SEQ, D_MODEL), 1.0),
        'x_sample': nrm(ks[1], (DEC_BATCH, DEC_SEQ, D_MODEL), 1.0),
        'state_wkv': nrm(ks[2], (L, DEC_BATCH, RWKV_HEADS, HEAD_DIM, HEAD_DIM), 0.3),
        'state_shift': nrm(ks[3], (L, DEC_BATCH, SHIFT_DIM), 1.0),
        'state_conv': nrm(ks[4], (L, DEC_BATCH, CONV_WIDTH - 1, CONV_DIM), 0.5),
        'c_prompt': nrm(ks[5], (BATCH, D_MODEL), 1.0),
        'c_sample': nrm(ks[6], (DEC_BATCH, D_MODEL), 1.0),
        'w_in': nrm(ks[7], (L, D_MODEL, PROJ_DIM), D_MODEL ** -0.5),
        'tm_mu': jax.random.uniform(ks[8], (L, SHIFT_DIM), f32),
        'w0': jax.random.uniform(ks[9], (L, RWKV_DIM), f32, -4.0, 0.0),
        'w2': nrm(ks[10], (L, DECAY_LORA, RWKV_DIM), 0.1),
        'a0': nrm(ks[11], (L, RWKV_DIM), 0.1),
        'a2': nrm(ks[12], (L, AAA_LORA, RWKV_DIM), 0.1),
        'g2': nrm(ks[13], (L, GATE_LORA, RWKV_DIM), GATE_LORA ** -0.5),
        'v0': nrm(ks[14], (L - 1, RWKV_DIM), 0.1),
        'v1': nrm(ks[15], (L - 1, RWKV_DIM, VRES_LORA), RWKV_DIM ** -0.5),
        'v2': nrm(ks[16], (L - 1, VRES_LORA, RWKV_DIM), 0.1),
        'k_k': 0.85 + nrm(ks[17], (L, RWKV_DIM), 0.05),
        'k_a': 1.0 + nrm(ks[18], (L, RWKV_DIM), 0.05),
        'r_k': nrm(ks[19], (L, RWKV_HEADS, HEAD_DIM), 0.1),
        'lnx_g': 1.0 + nrm(ks[20], (L, RWKV_DIM), 0.05),
        'lnx_b': nrm(ks[21], (L, RWKV_DIM), 0.02),
        'conv_w': nrm(ks[22], (L, CONV_WIDTH, CONV_DIM), CONV_WIDTH ** -0.5),
        'conv_b': nrm(ks[23], (L, CONV_DIM), 0.02),
        'conv_ln_g': 1.0 + nrm(ks[24], (L, CONV_DIM), 0.05),
        'conv_ln_b': nrm(ks[25], (L, CONV_DIM), 0.02),
        'w_out': nrm(ks[26], (L, MIX_DIM, D_MODEL), MIX_DIM ** -0.5 * DEEPNORM_BETA),
        'ada_w': nrm(ks[27], (L, D_MODEL, 6 * D_MODEL), D_MODEL ** -0.5),
        'ada_b': nrm(ks[28], (L, 6 * D_MODEL), 0.02),
        'ln1_g': 1.0 + nrm(ks[29], (L, D_MODEL), 0.05),
        'ln1_b': nrm(ks[30], (L, D_MODEL), 0.02),
        'ln2_g': 1.0 + nrm(ks[31], (L, D_MODEL), 0.05),
        'ln2_b': nrm(ks[32], (L, D_MODEL), 0.02),
        'router_w': nrm(ks[33], (L, D_MODEL, E), D_MODEL ** -0.5),
        'router_b': nrm(ks[34], (L, E), 0.01),
        'w_gu': nrm(ks[35], (L, E, D_MODEL, 2 * D_FF), D_MODEL ** -0.5),
        'b_gu': nrm(ks[36], (L, E, 2 * D_FF), 0.02),
        'w_dn': nrm(ks[37], (L, E, D_FF, D_MODEL), D_FF ** -0.5 * DEEPNORM_BETA),
        'b_dn': nrm(ks[38], (L, E, D_MODEL), 0.02),
    }


def reference(x_prompt, x_sample, state_wkv, state_shift, state_conv, c_prompt, c_sample,
              w_in, tm_mu, w0, w2, a0, a2, g2, v0, v1, v2, k_k, k_a, r_k, lnx_g, lnx_b,
              conv_w, conv_b, conv_ln_g, conv_ln_b, w_out, ada_w, ada_b,
              ln1_g, ln1_b, ln2_g, ln2_b, router_w, router_b, w_gu, b_gu, w_dn, b_dn):
    p = {'w_in': w_in, 'tm_mu': tm_mu, 'w0': w0, 'w2': w2, 'a0': a0, 'a2': a2, 'g2': g2,
         'v0': v0, 'v1': v1, 'v2': v2, 'k_k': k_k, 'k_a': k_a, 'r_k': r_k,
         'lnx_g': lnx_g, 'lnx_b': lnx_b, 'conv_w': conv_w, 'conv_b': conv_b,
         'conv_ln_g': conv_ln_g, 'conv_ln_b': conv_ln_b, 'w_out': w_out,
         'ada_w': ada_w, 'ada_b': ada_b, 'ln1_g': ln1_g, 'ln1_b': ln1_b,
         'ln2_g': ln2_g, 'ln2_b': ln2_b, 'router_w': router_w, 'router_b': router_b,
         'w_gu': w_gu, 'b_gu': b_gu, 'w_dn': w_dn, 'b_dn': b_dn}
    Bp = x_prompt.shape[0]
    dt = x_prompt.dtype
    wkv0 = jnp.zeros((DEPTH, Bp, RWKV_HEADS, HEAD_DIM, HEAD_DIM), dt)
    shift0 = jnp.zeros((DEPTH, Bp, SHIFT_DIM), dt)
    conv0 = jnp.zeros((DEPTH, Bp, CONV_WIDTH - 1, CONV_DIM), dt)
    y_prompt, wkv_p, shift_p, conv_p = _trunk(x_prompt, c_prompt, wkv0, shift0, conv0, p)
    y_sample, wkv_s, shift_s, conv_s = _trunk(x_sample, c_sample, state_wkv, state_shift, state_conv, p)
    return (y_prompt, y_sample, wkv_p, wkv_s, shift_p, shift_s, conv_p, conv_s)
```

```python
import functools

import jax
import jax.numpy as jnp
from jax import lax
from jax.experimental import pallas as pl
from jax.experimental.pallas import tpu as pltpu

F32 = jnp.float32
BF16 = jnp.bfloat16

D_MODEL = 1024
DEPTH = 4
HEAD_DIM = 64
RWKV_DIM = 512
RWKV_HEADS = RWKV_DIM // HEAD_DIM
CONV_DIM = 512
CONV_WIDTH = 31
CONV_HIST = CONV_WIDTH - 1
DECAY_LORA = 64
AAA_LORA = 64
GATE_LORA = 128
VRES_LORA = 32
SHIFT_DIM = 3 * RWKV_DIM + DECAY_LORA + AAA_LORA + GATE_LORA
PROJ_DIM = SHIFT_DIM + 2 * CONV_DIM
N_EXPERTS = 32
TOP_K = 4
D_FF = D_MODEL
SWIGLU_LIMIT = 7.0
SWIGLU_ALPHA = 1.702
LN_EPS = 1e-5
GN_EPS = 64e-5
DEEPNORM_ALPHA = (2 * DEPTH) ** 0.25

LANES = 128
SUBLANES = 8
ROW_TILE = 512
CHUNK = 64
EXPERT_ROWS = 256
VMEM_LIMIT = 56 * 1024 * 1024


def _params(n_axes=1, vmem=VMEM_LIMIT):
    return pltpu.CompilerParams(dimension_semantics=("arbitrary",) * n_axes, vmem_limit_bytes=vmem)


def _round_up(x, m):
    return (x + m - 1) // m * m


def _bdot(a, b):
    return jnp.dot(a.astype(BF16), b.astype(BF16), preferred_element_type=F32)


def _split3(x):
    hi = x.astype(BF16)
    r1 = x - hi.astype(F32)
    mid = r1.astype(BF16)
    lo = (r1 - mid.astype(F32)).astype(BF16)
    return hi, mid, lo


def _headsum(x, g):
    hi = x.astype(BF16)
    lo = (x - hi.astype(F32)).astype(BF16)
    return (jnp.dot(hi, g, preferred_element_type=F32) + jnp.dot(lo, g, preferred_element_type=F32))


def _layernorm(x):
    mu = jnp.mean(x, axis=-1, keepdims=True)
    xc = x - mu
    var = jnp.mean(xc * xc, axis=-1, keepdims=True)
    return xc * lax.rsqrt(var + LN_EPS)


def _ada_body(c_ref, w_ref, b_ref, o_ref):
    c = c_ref[...]
    cs = c * jax.nn.sigmoid(c)
    o_ref[...] = _bdot(cs, w_ref[...]) + b_ref[...]


def _ada(c_all, ada_w, ada_b):
    depth, _, width = ada_w.shape
    nb = c_all.shape[0]
    tn = 1536
    return pl.pallas_call(
        _ada_body,
        grid=(depth, width // tn),
        in_specs=[pl.BlockSpec((nb, D_MODEL), lambda l, j: (0, 0)),
                  pl.BlockSpec((None, D_MODEL, tn), lambda l, j: (l, 0, j)),
                  pl.BlockSpec((None, 1, tn), lambda l, j: (l, 0, j))],
        out_specs=pl.BlockSpec((None, nb, tn), lambda l, j: (l, 0, j)),
        out_shape=jax.ShapeDtypeStruct((depth, nb, width), F32),
        compiler_params=_params(2),
        name="ada_mod",
    )(c_all, ada_w, ada_b.reshape(depth, 1, width))


class _Group:
    def __init__(self, rows, seg_len, stride, mod_per_row):
        self.rows = rows
        self.seg_len = seg_len
        self.stride = stride
        self.mod_per_row = mod_per_row
        self.tm = min(ROW_TILE, rows)
        assert rows % self.tm == 0 and seg_len % self.tm == 0
        self.tiles = rows // self.tm
        self.tiles_per_seg = seg_len // self.tm
        self.nseg = rows // seg_len


def _mod_spec(grp, which):
    if grp.mod_per_row:
        return pl.BlockSpec((grp.tm, D_MODEL), lambda i: (i, which))
    tps = grp.tiles_per_seg
    return pl.BlockSpec((None, 1, D_MODEL), lambda i: (i // tps, 0, which))


def _row_spec(grp, width):
    return pl.BlockSpec((grp.tm, width), lambda i: (i, 0))


def _const_spec(shape):
    nd = len(shape)
    return pl.BlockSpec(shape, lambda i: (0,) * nd)


def _inproj_body(x_ref, sc_ref, sh_ref, w_ref, zs_ref, zc_ref):
    h = x_ref[...] * (1.0 + sc_ref[...]) + sh_ref[...]
    z = jnp.dot(h.astype(BF16), w_ref[...], preferred_element_type=F32)
    zs_ref[...] = z[:, :SHIFT_DIM]
    zc_ref[...] = z[:, SHIFT_DIM:]


def _inproj(grp, x, mod, w_in_b):
    return pl.pallas_call(
        _inproj_body,
        grid=(grp.tiles,),
        in_specs=[_row_spec(grp, D_MODEL), _mod_spec(grp, 1), _mod_spec(grp, 0),
                  _const_spec((D_MODEL, PROJ_DIM))],
        out_specs=[_row_spec(grp, SHIFT_DIM), _row_spec(grp, 2 * CONV_DIM)],
        out_shape=[jax.ShapeDtypeStruct((grp.rows, SHIFT_DIM), F32),
                   jax.ShapeDtypeStruct((grp.rows, 2 * CONV_DIM), F32)],
        compiler_params=_params(1),
        name="inproj",
    )(x, mod, mod, w_in_b)


def _prep_body(has_vres, grp, halo, *refs):
    it = iter(refs)
    zs_ref, sh0_ref = next(it), next(it)
    vf_ref = next(it) if has_vres else None
    mu_ref, w0_ref, a0_ref, kk_ref, ka_ref, wwa_ref, g2_ref = (next(it) for _ in range(7))
    if has_vres:
        v0_ref, v1_ref, v2_ref = next(it), next(it), next(it)
    gmat_ref = next(it)
    r_o, lw_o, k_o, v_o, an_o, bn_o, g_o = (next(it) for _ in range(7))
    ext = next(it)

    tm, s = grp.tm, grp.stride
    j = pl.program_id(0) % grp.tiles_per_seg

    @pl.when(j == 0)
    def _():
        ext[pl.ds(halo - s, s), :] = sh0_ref[...]

    zs = zs_ref[...]
    ext[pl.ds(halo, tm), :] = zs
    prev = ext[pl.ds(halo - s, tm), :]
    if grp.tiles_per_seg > 1:
        ext[pl.ds(0, halo), :] = ext[pl.ds(tm, halo), :]

    zm = zs + (prev - zs) * mu_ref[...]
    i1, i2, i3 = RWKV_DIM, 2 * RWKV_DIM, 3 * RWKV_DIM
    i5 = i3 + DECAY_LORA + AAA_LORA
    r, k, v = zm[:, :i1], zm[:, i1:i2], zm[:, i2:i3]
    zwa, zg = zm[:, i3:i5], zm[:, i5:]
    lane = lax.broadcasted_iota(jnp.int32, zwa.shape, 1)
    xwa = jnp.where(lane < DECAY_LORA, jnp.tanh(zwa), zwa)
    wa = _bdot(xwa, wwa_ref[...])
    w_in = w0_ref[...] + wa[:, :RWKV_DIM]
    neg = -w_in
    softplus = jnp.maximum(neg, 0.0) + jnp.log(1.0 + jnp.exp(-jnp.abs(neg)))
    w_log = -softplus - 0.5
    lw = -jnp.exp(w_log)
    a = jax.nn.sigmoid(a0_ref[...] + wa[:, RWKV_DIM:])
    g = _bdot(jax.nn.sigmoid(zg), g2_ref[...])
    if has_vres:
        vl = _bdot(_bdot(v, v1_ref[...]), v2_ref[...])
        vg = jax.nn.sigmoid(v0_ref[...] + vl)
        v = v + (vf_ref[...] - v) * vg
    kk = k * kk_ref[...]
    ss = _headsum(kk * kk, gmat_ref[...])
    kk = kk * lax.rsqrt(jnp.maximum(ss, 1e-24))
    k = k * (1.0 + (a - 1.0) * ka_ref[...])
    r_o[...] = r
    lw_o[...] = lw
    k_o[...] = k
    v_o[...] = v
    an_o[...] = -kk
    bn_o[...] = kk * a
    g_o[...] = g


def _prep(grp, zs, shift0, vfirst, wts, gmat):
    has_vres = vfirst is not None
    halo = _round_up(grp.stride, SUBLANES)
    s = grp.stride
    tps = grp.tiles_per_seg
    in_specs = [_row_spec(grp, SHIFT_DIM),
                pl.BlockSpec((None, s, SHIFT_DIM), lambda i: (i // tps, 0, 0))]
    args = [zs, shift0]
    if has_vres:
        in_specs.append(_row_spec(grp, RWKV_DIM))
        args.append(vfirst)
    names = ["mu", "w0", "a0", "k_k", "k_a", "wwa", "g2"] + (["v0", "v1", "v2"] if has_vres else [])
    for nm in names:
        in_specs.append(_const_spec(wts[nm].shape))
        args.append(wts[nm])
    in_specs.append(_const_spec(gmat.shape))
    args.append(gmat)
    out = jax.ShapeDtypeStruct((grp.rows, RWKV_DIM), F32)
    return pl.pallas_call(
        functools.partial(_prep_body, has_vres, grp, halo),
        grid=(grp.tiles,),
        in_specs=in_specs,
        out_specs=[_row_spec(grp, RWKV_DIM)] * 7,
        out_shape=[out] * 7,
        scratch_shapes=[pltpu.VMEM((halo + grp.tm, SHIFT_DIM), F32)],
        compiler_params=_params(1),
        name="rwkv_prep",
    )(*args)


def _nt(a, b):
    return lax.dot_general(a.astype(BF16), b.astype(BF16), (((1,), (1,)), ((), ())),
                           preferred_element_type=F32)


def _tn(a, b):
    return lax.dot_general(a.astype(BF16), b.astype(BF16), (((0,), (0,)), ((), ())),
                           preferred_element_type=F32)


def _wkv_chunk_body(nchunks, r_ref, lw_ref, k_ref, v_ref, a_ref, b_ref, s0_ref, y_ref, so_ref, st):
    c = pl.program_id(1)
    L, N = CHUNK, HEAD_DIM

    @pl.when(c == 0)
    def _():
        st[...] = s0_ref[...]

    row = lax.broadcasted_iota(jnp.int32, (L, L), 0)
    col = lax.broadcasted_iota(jnp.int32, (L, L), 1)
    strict = row > col
    incl = row >= col
    tri = jnp.where(incl, 1.0, 0.0).astype(BF16)
    eye = row == col
    sh = 3
    diag_blk = (row >> sh) == (col >> sh)
    off_masks = []
    while (1 << sh) < L:
        off_masks.append(((row >> sh) == (col >> sh) + 1) & ((row >> (sh + 1)) == (col >> (sh + 1))))
        sh += 1

    lw = lw_ref[...]
    h3 = _split3(lw)
    cl = (jnp.dot(tri, h3[0], preferred_element_type=F32) + jnp.dot(tri, h3[1], preferred_element_type=F32)
          + jnp.dot(tri, h3[2], preferred_element_type=F32))
    cl_last = cl[L - 1:L, :]
    ecl = jnp.exp(cl)
    encl = jnp.exp(-cl)
    dl = jnp.exp(cl_last - cl)
    p_last = jnp.exp(cl_last)
    an, bn, kx, rx, vx = a_ref[...], b_ref[...], k_ref[...], r_ref[...], v_ref[...]
    at_all = an * jnp.exp(cl - lw)
    rt_all = rx * ecl
    bt_all = bn * encl
    kt_all = kx * encl
    bh_all = bn * dl
    kh_all = kx * dl

    ys = []
    for h in range(RWKV_HEADS):
        sl = slice(h * N, (h + 1) * N)
        at, rt, bt, kt, bh, kh, vv = (t[:, sl] for t in (at_all, rt_all, bt_all, kt_all, bh_all, kh_all, vx))
        ar = jnp.concatenate([at, rt], axis=0)
        xb = _nt(ar, bt)
        xk = _nt(ar, kt)
        a_ab = jnp.where(strict, xb[:L], 0.0)
        a_ak = jnp.where(strict, xk[:L], 0.0)
        a_rb = jnp.where(incl, xb[L:], 0.0)
        a_rk = jnp.where(incl, xk[L:], 0.0)
        pw = jnp.where(diag_blk, a_ab, 0.0)
        tm = jnp.where(eye, 1.0, 0.0) + pw
        for _ in range(2):
            pw = _bdot(pw, pw)
            tm = tm + _bdot(tm, pw)
        for off in off_masks:
            tm = tm + _bdot(tm, _bdot(jnp.where(off, a_ab, 0.0), tm))
        w = _bdot(tm, at)
        u0 = _bdot(tm, _bdot(a_ak, vv))
        q = rt + _bdot(a_rb, w)
        y0 = _bdot(a_rb, u0) + _bdot(a_rk, vv)
        s_h = st[h]
        ys.append(_nt(q, s_h) + y0)
        cm = _tn(u0, bh) + _tn(vv, kh)
        st[h] = s_h * p_last[:, sl] + _bdot(s_h, _tn(w, bh)) + cm
    y_ref[...] = jnp.concatenate(ys, axis=1)

    @pl.when(c == nchunks - 1)
    def _():
        so_ref[...] = st[...]


def _wkv_chunked(nbatch, tlen, r, lw, k, v, an, bn, s0):
    nchunks = tlen // CHUNK
    tok = pl.BlockSpec((CHUNK, RWKV_DIM), lambda b, c: (b * nchunks + c, 0))
    stt = pl.BlockSpec((None, RWKV_HEADS, HEAD_DIM, HEAD_DIM), lambda b, c: (b, 0, 0, 0))
    return pl.pallas_call(
        functools.partial(_wkv_chunk_body, nchunks),
        grid=(nbatch, nchunks),
        in_specs=[tok] * 6 + [stt],
        out_specs=[tok, stt],
        out_shape=[jax.ShapeDtypeStruct((nbatch * tlen, RWKV_DIM), F32),
                   jax.ShapeDtypeStruct((nbatch, RWKV_HEADS, HEAD_DIM, HEAD_DIM), F32)],
        scratch_shapes=[pltpu.VMEM((RWKV_HEADS, HEAD_DIM, HEAD_DIM), F32)],
        compiler_params=_params(2),
        name="wkv_chunked",
    )(r, lw, k, v, an, bn, s0)


def _wkv_lane_body(tlen, r_ref, lw_ref, k_ref, v_ref, a_ref, b_ref, s_ref, y_ref, so_ref):
    N = HEAD_DIM
    nb = s_ref.shape[-1]

    def row(ref, t, j):
        return jnp.broadcast_to(ref[t, pl.ds(j, 1), :], (N, nb))

    def first(j, acc):
        return acc + s_ref[j] * row(a_ref, 0, j)

    sa = lax.fori_loop(0, N, first, jnp.zeros((N, nb), F32))
    for t in range(tlen):
        v_t = v_ref[t]
        src = s_ref if t == 0 else so_ref

        def body(j, carry, t=t, v_t=v_t, src=src, sa=sa):
            y, sa_next = carry
            sj = src[j] * jnp.exp(row(lw_ref, t, j)) + sa * row(b_ref, t, j) + v_t * row(k_ref, t, j)
            so_ref[j] = sj
            y = y + sj * row(r_ref, t, j)
            if t + 1 < tlen:
                sa_next = sa_next + sj * row(a_ref, t + 1, j)
            return y, sa_next

        zero = jnp.zeros((N, nb), F32)
        y, sa = lax.fori_loop(0, N, body, (zero, zero))
        y_ref[t] = y


def _wkv_lanes(tlen, r, lw, k, v, an, bn, s_t):
    nb = r.shape[-1]
    tok = pl.BlockSpec((tlen, HEAD_DIM, nb), lambda h: (0, h, 0))
    stt = pl.BlockSpec((None, HEAD_DIM, HEAD_DIM, nb), lambda h: (h, 0, 0, 0))
    return pl.pallas_call(
        functools.partial(_wkv_lane_body, tlen),
        grid=(RWKV_HEADS,),
        in_specs=[tok] * 6 + [stt],
        out_specs=[tok, stt],
        out_shape=[jax.ShapeDtypeStruct((tlen, RWKV_DIM, nb), F32),
                   jax.ShapeDtypeStruct((RWKV_HEADS, HEAD_DIM, HEAD_DIM, nb), F32)],
        compiler_params=_params(1),
        name="wkv_lanes",
    )(r, lw, k, v, an, bn, s_t)


def _conv_body(grp, halo, zc_ref, c0_ref, cw_ref, cb_ref, lg_ref, lb_ref, cv_ref, tail_ref, ext):
    tm, s = grp.tm, grp.stride
    hist = CONV_HIST * s
    j = pl.program_id(0) % grp.tiles_per_seg

    @pl.when(j == 0)
    def _():
        ext[pl.ds(halo - hist, hist), :] = c0_ref[...]

    zc = zc_ref[...]
    u = zc[:, :CONV_DIM] * jax.nn.sigmoid(zc[:, CONV_DIM:])
    ext[pl.ds(halo, tm), :] = u
    acc = jnp.broadcast_to(cb_ref[...], (tm, CONV_DIM))
    for w in range(CONV_WIDTH):
        acc = acc + cw_ref[pl.ds(w, 1), :] * ext[pl.ds(halo - hist + w * s, tm), :]
    y = _layernorm(acc) * lg_ref[...] + lb_ref[...]
    cv_ref[...] = y * jax.nn.sigmoid(y)
    tail_ref[...] = ext[pl.ds(halo + tm - hist, hist), :]
    if grp.tiles_per_seg > 1:
        ext[pl.ds(0, halo), :] = ext[pl.ds(tm, halo), :]


def _conv(grp, zc, conv0, cw, cb, lg, lb):
    hist = CONV_HIST * grp.stride
    halo = _round_up(hist, SUBLANES)
    tps = grp.tiles_per_seg
    hist_spec = pl.BlockSpec((None, hist, CONV_DIM), lambda i: (i // tps, 0, 0))
    return pl.pallas_call(
        functools.partial(_conv_body, grp, halo),
        grid=(grp.tiles,),
        in_specs=[_row_spec(grp, 2 * CONV_DIM), hist_spec,
                  _const_spec(cw.shape), _const_spec(cb.shape), _const_spec(lg.shape), _const_spec(lb.shape)],
        out_specs=[_row_spec(grp, CONV_DIM), hist_spec],
        out_shape=[jax.ShapeDtypeStruct((grp.rows, CONV_DIM), F32),
                   jax.ShapeDtypeStruct((grp.nseg, hist, CONV_DIM), F32)],
        scratch_shapes=[pltpu.VMEM((halo + grp.tm, CONV_DIM), F32)],
        compiler_params=_params(1),
        name="conv_group",
    )(zc, conv0, cw, cb, lg, lb)


def _mix_body(x_ref, y_ref, r_ref, k_ref, v_ref, g_ref, cv_ref, g1_ref, sc2_ref, sh2_ref,
              lnxg_ref, lnxb_ref, rk_ref, wout_ref, l1g_ref, l1b_ref, rw_ref, rb_ref, gmat_ref, *rest):
    x1_ref, h2_ref, ti_ref, gt_ref = rest[-4:]
    gm = gmat_ref[...]
    y = y_ref[...]
    inv = 1.0 / HEAD_DIM
    ym = _headsum(y, gm) * inv
    yc = y - ym
    yv = _headsum(yc * yc, gm) * inv
    yn = yc * lax.rsqrt(yv + GN_EPS) * lnxg_ref[...] + lnxb_ref[...]
    bonus = _headsum(r_ref[...] * k_ref[...] * rk_ref[...], gm) * v_ref[...]
    yo = (yn + bonus) * g_ref[...]
    mix = (jnp.dot(yo.astype(BF16), wout_ref[pl.ds(0, RWKV_DIM), :], preferred_element_type=F32)
           + jnp.dot(cv_ref[...].astype(BF16), wout_ref[pl.ds(RWKV_DIM, CONV_DIM), :],
                     preferred_element_type=F32))
    xa = DEEPNORM_ALPHA * x_ref[...] + g1_ref[...] * mix
    x1 = _layernorm(xa) * l1g_ref[...] + l1b_ref[...]
    x1_ref[...] = x1
    h2 = x1 * (1.0 + sc2_ref[...]) + sh2_ref[...]
    h2_ref[...] = h2
    logits = jnp.dot(h2, rw_ref[...], precision=lax.Precision.HIGHEST,
                     preferred_element_type=F32) + rb_ref[...]
    lane = lax.broadcasted_iota(jnp.int32, logits.shape, 1).astype(F32)
    cur = logits
    vals, idxs = [], []
    for _ in range(TOP_K):
        m = jnp.max(cur, axis=-1, keepdims=True)
        idx = jnp.min(jnp.where(cur == m, lane, float(LANES)), axis=-1, keepdims=True)
        vals.append(m)
        idxs.append(idx)
        cur = jnp.where(lane == idx, -jnp.inf, cur)
    es = [jnp.exp(vv - vals[0]) for vv in vals]
    den = es[0] + es[1] + es[2] + es[3]
    ti = jnp.zeros_like(logits)
    gt = jnp.zeros_like(logits)
    for kk in range(TOP_K):
        ti = jnp.where(lane == float(kk), idxs[kk], ti)
        gt = jnp.where(lane == float(kk), es[kk] / den, gt)
    ti_ref[...] = ti.astype(jnp.int32)
    gt_ref[...] = gt


def _mix(grp, x, y, r, k, v, g, cv, mod, wts, gmat, h2_buf, n_tok, row_offset):
    toff = row_offset // grp.tm
    assert row_offset % grp.tm == 0
    half = _row_spec(grp, RWKV_DIM)
    names = ["lnx_g", "lnx_b", "r_k", "w_out", "ln1_g", "ln1_b", "router_w", "router_b"]
    in_specs = ([_row_spec(grp, D_MODEL)] + [half] * 6 + [_mod_spec(grp, 2), _mod_spec(grp, 4), _mod_spec(grp, 3)]
                + [_const_spec(wts[nm].shape) for nm in names] + [_const_spec(gmat.shape)])
    args = [x, y, r, k, v, g, cv, mod, mod, mod] + [wts[nm] for nm in names] + [gmat]
    aliases = {}
    if h2_buf is not None:
        in_specs.append(pl.BlockSpec(memory_space=pl.ANY))
        args.append(h2_buf)
        aliases = {len(args) - 1: 1}
    return pl.pallas_call(
        _mix_body,
        grid=(grp.tiles,),
        in_specs=in_specs,
        out_specs=[_row_spec(grp, D_MODEL),
                   pl.BlockSpec((grp.tm, D_MODEL), lambda i: (i + toff, 0)),
                   _row_spec(grp, LANES), _row_spec(grp, LANES)],
        out_shape=[jax.ShapeDtypeStruct((grp.rows, D_MODEL), F32),
                   jax.ShapeDtypeStruct((n_tok, D_MODEL), F32),
                   jax.ShapeDtypeStruct((grp.rows, LANES), jnp.int32),
                   jax.ShapeDtypeStruct((grp.rows, LANES), F32)],
        input_output_aliases=aliases,
        compiler_params=_params(1),
        name="mix_out",
    )(*args)


def _expert_body(dump_start, be_ref, nu_ref, tok_ref, nxt_ref, dst_ref, h_hbm, wgu_ref, bgu_ref, wdn_ref,
                 bdn_ref, y_hbm, xbuf, obuf, wgu_b, wdn_b, gsem, ssem):
    i = pl.program_id(0)
    n_used = nu_ref[0]
    slot = i % 2
    rows = EXPERT_ROWS

    def gather(idx_ref, sl):
        def body(rr, carry):
            t = idx_ref[0, rr]
            pltpu.make_async_copy(h_hbm.at[pl.ds(t, 1), :], xbuf.at[sl, pl.ds(rr, 1), :], gsem.at[sl]).start()
            return carry
        lax.fori_loop(0, rows, body, 0)

    def scatter(sl):
        def body(rr, carry):
            d = dst_ref[0, rr]
            pltpu.make_async_copy(obuf.at[sl, pl.ds(rr, 1), :], y_hbm.at[pl.ds(d, 1), :], ssem.at[sl]).start()
            return carry
        lax.fori_loop(0, rows, body, 0)

    def wait_gather(sl):
        pltpu.make_async_copy(xbuf.at[sl], xbuf.at[sl], gsem.at[sl]).wait()

    def wait_scatter(sl):
        pltpu.make_async_copy(obuf.at[sl], obuf.at[sl], ssem.at[sl]).wait()

    @pl.when(i == 0)
    def _():
        gather(tok_ref, 0)
        obuf[...] = jnp.zeros(obuf.shape, F32)
        for sl in range(2):
            cp = pltpu.make_async_copy(obuf.at[sl], y_hbm.at[pl.ds(dump_start + sl * rows, rows), :], ssem.at[sl])
            cp.start()
            cp.wait()

    @pl.when(i + 1 < n_used)
    def _():
        gather(nxt_ref, 1 - slot)

    @pl.when(i < n_used)
    def _():
        wait_gather(slot)
        prev_e = be_ref[jnp.maximum(i - 1, 0)]

        @pl.when((i == 0) | (be_ref[i] != prev_e))
        def _():
            wgu_b[...] = wgu_ref[...].astype(BF16)
            wdn_b[...] = wdn_ref[...].astype(BF16)

        @pl.when(i >= 2)
        def _():
            wait_scatter(slot)

        x = xbuf[slot].astype(BF16)
        gu = jnp.dot(x, wgu_b[...], preferred_element_type=F32) + bgu_ref[...]
        gate = jnp.minimum(gu[:, :D_FF], SWIGLU_LIMIT)
        lin = jnp.clip(gu[:, D_FF:], -SWIGLU_LIMIT, SWIGLU_LIMIT)
        act = gate * jax.nn.sigmoid(SWIGLU_ALPHA * gate) * (lin + 1.0)
        obuf[slot] = jnp.dot(act.astype(BF16), wdn_b[...], preferred_element_type=F32) + bdn_ref[...]
        scatter(slot)

        @pl.when(i == n_used - 1)
        def _():
            wait_scatter(slot)

            @pl.when(i >= 1)
            def _():
                wait_scatter(1 - slot)


def _experts(layer, h2, block_e, n_used, row_tok, row_dst, w_gu, b_gu, w_dn, b_dn, n_out_rows):
    n_blocks = block_e.shape[0]
    rows = EXPERT_ROWS
    smem_blk = lambda f: pl.BlockSpec((None, 1, rows), f, memory_space=pltpu.SMEM)
    grid_spec = pltpu.PrefetchScalarGridSpec(
        num_scalar_prefetch=2,
        grid=(n_blocks,),
        in_specs=[smem_blk(lambda i, be, nu: (i, 0, 0)),
                  smem_blk(lambda i, be, nu: (jnp.minimum(i + 1, n_blocks - 1), 0, 0)),
                  smem_blk(lambda i, be, nu: (i, 0, 0)),
                  pl.BlockSpec(memory_space=pl.ANY),
                  pl.BlockSpec((None, None, D_MODEL, 2 * D_FF), lambda i, be, nu: (layer, be[i], 0, 0)),
                  pl.BlockSpec((None, None, 1, 2 * D_FF), lambda i, be, nu: (layer, be[i], 0, 0)),
                  pl.BlockSpec((None, None, D_FF, D_MODEL), lambda i, be, nu: (layer, be[i], 0, 0)),
                  pl.BlockSpec((None, None, 1, D_MODEL), lambda i, be, nu: (layer, be[i], 0, 0))],
        out_specs=pl.BlockSpec(memory_space=pl.ANY),
        scratch_shapes=[pltpu.VMEM((2, rows, D_MODEL), F32), pltpu.VMEM((2, rows, D_MODEL), F32),
                        pltpu.VMEM((D_MODEL, 2 * D_FF), BF16), pltpu.VMEM((D_FF, D_MODEL), BF16),
                        pltpu.SemaphoreType.DMA((2,)), pltpu.SemaphoreType.DMA((2,))])
    return pl.pallas_call(
        functools.partial(_expert_body, n_out_rows - 2 * rows),
        grid_spec=grid_spec,
        out_shape=jax.ShapeDtypeStruct((n_out_rows, D_MODEL), F32),
        compiler_params=pltpu.CompilerParams(dimension_semantics=("arbitrary",), vmem_limit_bytes=VMEM_LIMIT,
                                             has_side_effects=True),
        name="experts",
    )(block_e, n_used, row_tok, row_tok, row_dst, h2, w_gu, b_gu, w_dn, b_dn)


def _route(top_i, n_tok):
    rows = EXPERT_ROWS
    tk = n_tok * TOP_K
    n_blocks = -(-tk // rows) + N_EXPERTS
    n_rows = n_blocks * rows
    flat_e = top_i.reshape(tk)
    order = jnp.argsort(flat_e, stable=True).astype(jnp.int32)
    e_sorted = flat_e[order]
    counts = jnp.bincount(flat_e, length=N_EXPERTS).astype(jnp.int32)
    padded = (counts + rows - 1) // rows * rows
    padded_end = jnp.cumsum(padded)
    padded_start = padded_end - padded
    start = jnp.cumsum(counts) - counts
    dest = padded_start[e_sorted] + jnp.arange(tk, dtype=jnp.int32) - start[e_sorted]
    tok_sorted = order // TOP_K
    slot_sorted = order % TOP_K
    ridx = jnp.arange(n_rows, dtype=jnp.int32)
    dump = tk + ((ridx // rows) % 2) * rows + ridx % rows
    row_tok = jnp.zeros((n_rows,), jnp.int32).at[dest].set(tok_sorted)
    row_dst = dump.at[dest].set(slot_sorted * n_tok + tok_sorted)
    block_start = jnp.arange(n_blocks, dtype=jnp.int32) * rows
    block_e = jnp.minimum(jnp.searchsorted(padded_end, block_start, side='right'), N_EXPERTS - 1).astype(jnp.int32)
    n_used = (padded_end[-1] // rows).astype(jnp.int32).reshape(1)
    return (block_e, n_used, row_tok.reshape(n_blocks, 1, rows), row_dst.reshape(n_blocks, 1, rows),
            tk + 2 * rows)


def _combine_body(x1_ref, gt_ref, y0_ref, y1_ref, y2_ref, y3_ref, g2_ref, lg_ref, lb_ref, o_ref):
    gt = gt_ref[...]
    ff = (gt[:, 0:1] * y0_ref[...] + gt[:, 1:2] * y1_ref[...] + gt[:, 2:3] * y2_ref[...]
          + gt[:, 3:4] * y3_ref[...])
    xa = DEEPNORM_ALPHA * x1_ref[...] + g2_ref[...] * ff
    o_ref[...] = _layernorm(xa) * lg_ref[...] + lb_ref[...]


def _combine(grp, x1, gates, ybuf, n_tok, row_offset, mod, lg, lb):
    toff = row_offset // grp.tm
    tiles_all = n_tok // grp.tm
    assert n_tok % grp.tm == 0

    def yspec(slot):
        return pl.BlockSpec((grp.tm, D_MODEL), lambda i: (slot * tiles_all + toff + i, 0))

    return pl.pallas_call(
        _combine_body,
        grid=(grp.tiles,),
        in_specs=[_row_spec(grp, D_MODEL), _row_spec(grp, LANES), yspec(0), yspec(1), yspec(2), yspec(3),
                  _mod_spec(grp, 5), _const_spec(lg.shape), _const_spec(lb.shape)],
        out_specs=_row_spec(grp, D_MODEL),
        out_shape=jax.ShapeDtypeStruct((grp.rows, D_MODEL), F32),
        compiler_params=_params(1),
        name="combine_ln",
    )(x1, gates, ybuf, ybuf, ybuf, ybuf, mod, lg, lb)


def kernel(x_prompt, x_sample, state_wkv, state_shift, state_conv, c_prompt, c_sample, w_in, tm_mu, w0, w2, a0, a2, g2, v0, v1, v2, k_k, k_a, r_k, lnx_g, lnx_b, conv_w, conv_b, conv_ln_g, conv_ln_b, w_out, ada_w, ada_b, ln1_g, ln1_b, ln2_g, ln2_b, router_w, router_b, w_gu, b_gu, w_dn, b_dn):
    bp, tp, _ = x_prompt.shape
    bs, ts, _ = x_sample.shape
    depth = w_in.shape[0]
    n_p, n_s = bp * tp, bs * ts
    n_tok = n_p + n_s
    gp = _Group(n_p, tp, 1, False)
    gs = _Group(n_s, n_s, bs, True)

    mod_all = _ada(jnp.concatenate([c_prompt, c_sample], axis=0), ada_w, ada_b)

    w_in_b = w_in.astype(BF16)
    w_out_b = w_out.astype(BF16)
    hid = jnp.arange(RWKV_DIM) // HEAD_DIM
    gmat = (hid[:, None] == hid[None, :]).astype(BF16)
    zero_wa = jnp.zeros((depth, DECAY_LORA, RWKV_DIM), F32)
    wwa = jnp.concatenate([jnp.concatenate([w2, zero_wa], axis=2),
                           jnp.concatenate([zero_wa, a2], axis=2)], axis=1).astype(BF16)
    v1p = jnp.pad(v1, ((0, 0), (0, 0), (0, LANES - VRES_LORA))).astype(BF16)
    v2p = jnp.pad(v2, ((0, 0), (0, LANES - VRES_LORA), (0, 0))).astype(BF16)
    rw_p = jnp.pad(router_w, ((0, 0), (0, 0), (0, LANES - N_EXPERTS)))
    rb_p = jnp.pad(router_b, ((0, 0), (0, LANES - N_EXPERTS)), constant_values=-1e30)
    b_gu4 = b_gu.reshape(depth, N_EXPERTS, 1, 2 * D_FF)
    b_dn4 = b_dn.reshape(depth, N_EXPERTS, 1, D_MODEL)

    def row2(a):
        return a.reshape(1, -1)

    x_p = x_prompt.reshape(n_p, D_MODEL)
    x_s = jnp.transpose(x_sample, (1, 0, 2)).reshape(n_s, D_MODEL)
    zeros_wkv = jnp.zeros((bp, RWKV_HEADS, HEAD_DIM, HEAD_DIM), F32)
    zeros_shift = jnp.zeros((bp, 1, SHIFT_DIM), F32)
    zeros_conv = jnp.zeros((bp, CONV_HIST, CONV_DIM), F32)
    wkv_s_t = jnp.transpose(state_wkv, (0, 2, 4, 3, 1))

    vf_p = vf_s = None
    wkv_p_out, wkv_s_out, shift_p_out, shift_s_out, conv_p_out, conv_s_out = [], [], [], [], [], []
    for l in range(depth):
        mod_p = mod_all[l, :bp].reshape(bp, 1, 6 * D_MODEL)
        mod_s = jnp.tile(mod_all[l, bp:], (ts, 1))
        wts = {"mu": row2(tm_mu[l]), "w0": row2(w0[l]), "a0": row2(a0[l]), "k_k": row2(k_k[l]),
               "k_a": row2(k_a[l]), "wwa": wwa[l], "g2": g2[l].astype(BF16),
               "lnx_g": row2(lnx_g[l]), "lnx_b": row2(lnx_b[l]), "r_k": row2(r_k[l]), "w_out": w_out_b[l],
               "ln1_g": row2(ln1_g[l]), "ln1_b": row2(ln1_b[l]), "router_w": rw_p[l], "router_b": row2(rb_p[l])}
        if l > 0:
            wts.update({"v0": row2(v0[l - 1]), "v1": v1p[l - 1], "v2": v2p[l - 1]})

        if l == 0:
            h2_buf = jnp.zeros((n_tok, D_MODEL), F32)
        per_group = []
        for grp, x, mod, is_p in ((gp, x_p, mod_p, True), (gs, x_s, mod_s, False)):
            zs, zc = _inproj(grp, x, mod, w_in_b[l])
            if is_p:
                shift0, conv0 = zeros_shift, zeros_conv
            else:
                shift0 = state_shift[l][None]
                conv0 = jnp.transpose(state_conv[l], (1, 0, 2)).reshape(1, CONV_HIST * bs, CONV_DIM)
            vfirst = vf_p if is_p else vf_s
            r, lw, k, v, an, bn, g = _prep(grp, zs, shift0, vfirst if l > 0 else None, wts, gmat)
            if l == 0:
                if is_p:
                    vf_p = v
                else:
                    vf_s = v
            if is_p:
                y, s_new = _wkv_chunked(bp, tp, r, lw, k, v, an, bn, zeros_wkv)
                wkv_p_out.append(s_new)
                shift_p_out.append(zs.reshape(bp, tp, SHIFT_DIM)[:, -1])
            else:
                tmaj = lambda a: jnp.transpose(a.reshape(ts, bs, RWKV_DIM), (0, 2, 1))
                y_t, s_new = _wkv_lanes(ts, tmaj(r), tmaj(lw), tmaj(k), tmaj(v), tmaj(an), tmaj(bn), wkv_s_t[l])
                y = jnp.transpose(y_t, (0, 2, 1)).reshape(n_s, RWKV_DIM)
                wkv_s_out.append(jnp.transpose(s_new, (3, 0, 2, 1)))
                shift_s_out.append(zs[(ts - 1) * bs:])
            cv, tail = _conv(grp, zc, conv0, conv_w[l], row2(conv_b[l]), row2(conv_ln_g[l]), row2(conv_ln_b[l]))
            if is_p:
                conv_p_out.append(tail)
            else:
                conv_s_out.append(jnp.transpose(tail.reshape(CONV_HIST, bs, CONV_DIM), (1, 0, 2)))
            off = 0 if is_p else n_p
            x1, h2_buf, ti, gt = _mix(grp, x, y, r, k, v, g, cv, mod, wts, gmat, h2_buf, n_tok, off)
            per_group.append((grp, x1, ti, gt, mod, off))

        top_i = jnp.concatenate([pg[2][:, :TOP_K] for pg in per_group], axis=0)
        block_e, n_used, row_tok, row_dst, n_out_rows = _route(top_i, n_tok)
        ybuf = _experts(l, h2_buf, block_e, n_used, row_tok, row_dst, w_gu, b_gu4, w_dn, b_dn4, n_out_rows)
        outs = []
        for grp, x1, ti, gt, mod, off in per_group:
            outs.append(_combine(grp, x1, gt, ybuf, n_tok, off, mod, row2(ln2_g[l]), row2(ln2_b[l])))
        x_p, x_s = outs

    y_prompt = x_p.reshape(bp, tp, D_MODEL)
    y_sample = jnp.transpose(x_s.reshape(ts, bs, D_MODEL), (1, 0, 2))
    return (y_prompt, y_sample, jnp.stack(wkv_p_out), jnp.stack(wkv_s_out), jnp.stack(shift_p_out),
            jnp.stack(shift_s_out), jnp.stack(conv_p_out), jnp.stack(conv_s_out))
```

```python
import functools

import jax
import jax.numpy as jnp
from jax import lax
from jax.experimental import pallas as pl
from jax.experimental.pallas import tpu as pltpu

F32 = jnp.float32
BF16 = jnp.bfloat16

D_MODEL = 1024
DEPTH = 4
HEAD_DIM = 64
RWKV_DIM = 512
RWKV_HEADS = RWKV_DIM // HEAD_DIM
CONV_DIM = 512
CONV_WIDTH = 31
CONV_HIST = CONV_WIDTH - 1
DECAY_LORA = 64
AAA_LORA = 64
GATE_LORA = 128
VRES_LORA = 32
SHIFT_DIM = 3 * RWKV_DIM + DECAY_LORA + AAA_LORA + GATE_LORA
PROJ_DIM = SHIFT_DIM + 2 * CONV_DIM
N_EXPERTS = 32
TOP_K = 4
D_FF = D_MODEL
SWIGLU_LIMIT = 7.0
SWIGLU_ALPHA = 1.702
LN_EPS = 1e-5
GN_EPS = 64e-5
DEEPNORM_ALPHA = (2 * DEPTH) ** 0.25

LANES = 128
SUBLANES = 8
ROW_TILE = 512
CHUNK = 64
EXPERT_ROWS = 256
VMEM_LIMIT = 56 * 1024 * 1024


def _params(n_axes=1, vmem=VMEM_LIMIT):
    return pltpu.CompilerParams(dimension_semantics=("arbitrary",) * n_axes, vmem_limit_bytes=vmem)


def _round_up(x, m):
    return (x + m - 1) // m * m


def _bdot(a, b):
    return jnp.dot(a.astype(BF16), b.astype(BF16), preferred_element_type=F32)


def _split3(x):
    hi = x.astype(BF16)
    r1 = x - hi.astype(F32)
    mid = r1.astype(BF16)
    lo = (r1 - mid.astype(F32)).astype(BF16)
    return hi, mid, lo


def _headsum(x, g):
    hi = x.astype(BF16)
    lo = (x - hi.astype(F32)).astype(BF16)
    return (jnp.dot(hi, g, preferred_element_type=F32) + jnp.dot(lo, g, preferred_element_type=F32))


def _layernorm(x):
    mu = jnp.mean(x, axis=-1, keepdims=True)
    xc = x - mu
    var = jnp.mean(xc * xc, axis=-1, keepdims=True)
    return xc * lax.rsqrt(var + LN_EPS)


def _ada_body(c_ref, w_ref, b_ref, o_ref):
    c = c_ref[...]
    cs = c * jax.nn.sigmoid(c)
    o_ref[...] = _bdot(cs, w_ref[...]) + b_ref[...]


def _ada(c_all, ada_w, ada_b):
    depth, _, width = ada_w.shape
    nb = c_all.shape[0]
    tn = 1536
    return pl.pallas_call(
        _ada_body,
        grid=(depth, width // tn),
        in_specs=[pl.BlockSpec((nb, D_MODEL), lambda l, j: (0, 0)),
                  pl.BlockSpec((None, D_MODEL, tn), lambda l, j: (l, 0, j)),
                  pl.BlockSpec((None, 1, tn), lambda l, j: (l, 0, j))],
        out_specs=pl.BlockSpec((None, nb, tn), lambda l, j: (l, 0, j)),
        out_shape=jax.ShapeDtypeStruct((depth, nb, width), F32),
        compiler_params=_params(2),
        name="ada_mod",
    )(c_all, ada_w, ada_b.reshape(depth, 1, width))


class _Group:
    def __init__(self, rows, seg_len, stride, mod_per_row):
        self.rows = rows
        self.seg_len = seg_len
        self.stride = stride
        self.mod_per_row = mod_per_row
        self.tm = min(ROW_TILE, rows)
        assert rows % self.tm == 0 and seg_len % self.tm == 0
        self.tiles = rows // self.tm
        self.tiles_per_seg = seg_len // self.tm
        self.nseg = rows // seg_len


def _mod_spec(grp, which):
    if grp.mod_per_row:
        return pl.BlockSpec((grp.tm, D_MODEL), lambda i: (i, which))
    tps = grp.tiles_per_seg
    return pl.BlockSpec((None, 1, D_MODEL), lambda i: (i // tps, 0, which))


def _row_spec(grp, width):
    return pl.BlockSpec((grp.tm, width), lambda i: (i, 0))


def _const_spec(shape):
    nd = len(shape)
    return pl.BlockSpec(shape, lambda i: (0,) * nd)


def _inproj_body(x_ref, sc_ref, sh_ref, w_ref, zs_ref, zc_ref):
    h = x_ref[...] * (1.0 + sc_ref[...]) + sh_ref[...]
    z = jnp.dot(h.astype(BF16), w_ref[...], preferred_element_type=F32)
    zs_ref[...] = z[:, :SHIFT_DIM]
    zc_ref[...] = z[:, SHIFT_DIM:]


def _inproj(grp, x, mod, w_in_b):
    return pl.pallas_call(
        _inproj_body,
        grid=(grp.tiles,),
        in_specs=[_row_spec(grp, D_MODEL), _mod_spec(grp, 1), _mod_spec(grp, 0),
                  _const_spec((D_MODEL, PROJ_DIM))],
        out_specs=[_row_spec(grp, SHIFT_DIM), _row_spec(grp, 2 * CONV_DIM)],
        out_shape=[jax.ShapeDtypeStruct((grp.rows, SHIFT_DIM), F32),
                   jax.ShapeDtypeStruct((grp.rows, 2 * CONV_DIM), F32)],
        compiler_params=_params(1),
        name="inproj",
    )(x, mod, mod, w_in_b)


def _prep_body(has_vres, grp, halo, *refs):
    it = iter(refs)
    zs_ref, sh0_ref = next(it), next(it)
    vf_ref = next(it) if has_vres else None
    mu_ref, w0_ref, a0_ref, kk_ref, ka_ref, wwa_ref, g2_ref = (next(it) for _ in range(7))
    if has_vres:
        v0_ref, v1_ref, v2_ref = next(it), next(it), next(it)
    gmat_ref = next(it)
    r_o, lw_o, k_o, v_o, an_o, bn_o, g_o = (next(it) for _ in range(7))
    ext = next(it)

    tm, s = grp.tm, grp.stride
    j = pl.program_id(0) % grp.tiles_per_seg

    @pl.when(j == 0)
    def _():
        ext[pl.ds(halo - s, s), :] = sh0_ref[...]

    zs = zs_ref[...]
    ext[pl.ds(halo, tm), :] = zs
    prev = ext[pl.ds(halo - s, tm), :]
    if grp.tiles_per_seg > 1:
        ext[pl.ds(0, halo), :] = ext[pl.ds(tm, halo), :]

    zm = zs + (prev - zs) * mu_ref[...]
    i1, i2, i3 = RWKV_DIM, 2 * RWKV_DIM, 3 * RWKV_DIM
    i5 = i3 + DECAY_LORA + AAA_LORA
    r, k, v = zm[:, :i1], zm[:, i1:i2], zm[:, i2:i3]
    zwa, zg = zm[:, i3:i5], zm[:, i5:]
    lane = lax.broadcasted_iota(jnp.int32, zwa.shape, 1)
    xwa = jnp.where(lane < DECAY_LORA, jnp.tanh(zwa), zwa)
    wa = _bdot(xwa, wwa_ref[...])
    w_in = w0_ref[...] + wa[:, :RWKV_DIM]
    neg = -w_in
    softplus = jnp.maximum(neg, 0.0) + jnp.log(1.0 + jnp.exp(-jnp.abs(neg)))
    w_log = -softplus - 0.5
    lw = -jnp.exp(w_log)
    a = jax.nn.sigmoid(a0_ref[...] + wa[:, RWKV_DIM:])
    g = _bdot(jax.nn.sigmoid(zg), g2_ref[...])
    if has_vres:
        vl = _bdot(_bdot(v, v1_ref[...]), v2_ref[...])
        vg = jax.nn.sigmoid(v0_ref[...] + vl)
        v = v + (vf_ref[...] - v) * vg
    kk = k * kk_ref[...]
    ss = _headsum(kk * kk, gmat_ref[...])
    kk = kk * lax.rsqrt(jnp.maximum(ss, 1e-24))
    k = k * (1.0 + (a - 1.0) * ka_ref[...])
    r_o[...] = r
    lw_o[...] = lw
    k_o[...] = k
    v_o[...] = v
    an_o[...] = -kk
    bn_o[...] = kk * a
    g_o[...] = g


def _prep(grp, zs, shift0, vfirst, wts, gmat):
    has_vres = vfirst is not None
    halo = _round_up(grp.stride, SUBLANES)
    s = grp.stride
    tps = grp.tiles_per_seg
    in_specs = [_row_spec(grp, SHIFT_DIM),
                pl.BlockSpec((None, s, SHIFT_DIM), lambda i: (i // tps, 0, 0))]
    args = [zs, shift0]
    if has_vres:
        in_specs.append(_row_spec(grp, RWKV_DIM))
        args.append(vfirst)
    names = ["mu", "w0", "a0", "k_k", "k_a", "wwa", "g2"] + (["v0", "v1", "v2"] if has_vres else [])
    for nm in names:
        in_specs.append(_const_spec(wts[nm].shape))
        args.append(wts[nm])
    in_specs.append(_const_spec(gmat.shape))
    args.append(gmat)
    out = jax.ShapeDtypeStruct((grp.rows, RWKV_DIM), F32)
    return pl.pallas_call(
        functools.partial(_prep_body, has_vres, grp, halo),
        grid=(grp.tiles,),
        in_specs=in_specs,
        out_specs=[_row_spec(grp, RWKV_DIM)] * 7,
        out_shape=[out] * 7,
        scratch_shapes=[pltpu.VMEM((halo + grp.tm, SHIFT_DIM), F32)],
        compiler_params=_params(1),
        name="rwkv_prep",
    )(*args)


def _nt(a, b):
    return lax.dot_general(a.astype(BF16), b.astype(BF16), (((1,), (1,)), ((), ())),
                           preferred_element_type=F32)


def _tn(a, b):
    return lax.dot_general(a.astype(BF16), b.astype(BF16), (((0,), (0,)), ((), ())),
                           preferred_element_type=F32)


def _wkv_chunk_body(nchunks, r_ref, lw_ref, k_ref, v_ref, a_ref, b_ref, s0_ref, y_ref, so_ref, st):
    c = pl.program_id(1)
    L, N = CHUNK, HEAD_DIM

    @pl.when(c == 0)
    def _():
        st[...] = s0_ref[...]

    P2, W2 = 2 * L, 2 * N
    assert L == N and W2 == LANES
    trow = lax.broadcasted_iota(jnp.int32, (L, L), 0)
    tcol = lax.broadcasted_iota(jnp.int32, (L, L), 1)
    tri = jnp.where(trow >= tcol, 1.0, 0.0).astype(BF16)
    row = lax.broadcasted_iota(jnp.int32, (P2, P2), 0)
    col = lax.broadcasted_iota(jnp.int32, (P2, P2), 1)
    lg = L.bit_length() - 1
    same_head = (row >> lg) == (col >> lg)
    strict = same_head & (row > col)
    incl = same_head & (row >= col)
    eye = row == col
    sh = 3
    diag_blk = (row >> sh) == (col >> sh)
    off_masks = []
    while (1 << sh) < L:
        off_masks.append(((row >> sh) == (col >> sh) + 1) & ((row >> (sh + 1)) == (col >> (sh + 1))))
        sh += 1

    lw = lw_ref[...]
    h3 = _split3(lw)
    cl = (jnp.dot(tri, h3[0], preferred_element_type=F32) + jnp.dot(tri, h3[1], preferred_element_type=F32)
          + jnp.dot(tri, h3[2], preferred_element_type=F32))
    cl_last = cl[L - 1:L, :]
    ecl = jnp.exp(cl)
    encl = jnp.exp(-cl)
    dl = jnp.exp(cl_last - cl)
    p_last = jnp.exp(cl_last)
    an, bn, kx, rx, vx = a_ref[...], b_ref[...], k_ref[...], r_ref[...], v_ref[...]
    lane_lo = lax.broadcasted_iota(jnp.int32, (L, W2), 1) < N

    def stack(x, p):
        xp = x[:, p * W2:(p + 1) * W2]
        return jnp.concatenate([jnp.where(lane_lo, xp, 0.0), jnp.where(lane_lo, 0.0, xp)], axis=0)

    pairs = range(RWKV_HEADS // 2)
    at = [stack(an * jnp.exp(cl - lw), p).astype(BF16) for p in pairs]
    rt = [stack(rx * ecl, p) for p in pairs]
    bt = [stack(bn * encl, p).astype(BF16) for p in pairs]
    kt = [stack(kx * encl, p).astype(BF16) for p in pairs]
    bh = [stack(bn * dl, p).astype(BF16) for p in pairs]
    kh = [stack(kx * dl, p).astype(BF16) for p in pairs]
    vv = [stack(vx, p).astype(BF16) for p in pairs]

    ar = [jnp.concatenate([at[p], rt[p].astype(BF16)], axis=0) for p in pairs]
    xb = [_nt(ar[p], bt[p]) for p in pairs]
    xk = [_nt(ar[p], kt[p]) for p in pairs]
    a_ab = [jnp.where(strict, xb[p][:P2], 0.0) for p in pairs]
    a_ak = [jnp.where(strict, xk[p][:P2], 0.0) for p in pairs]
    a_rb = [jnp.where(incl, xb[p][P2:], 0.0).astype(BF16) for p in pairs]
    a_rk = [jnp.where(incl, xk[p][P2:], 0.0) for p in pairs]
    pw = [jnp.where(diag_blk, a_ab[p], 0.0) for p in pairs]
    tm = [jnp.where(eye, 1.0, 0.0) + pw[p] for p in pairs]
    for _ in range(2):
        pw = [_bdot(pw[p], pw[p]) for p in pairs]
        tm = [tm[p] + _bdot(tm[p], pw[p]) for p in pairs]
    for off in off_masks:
        t1 = [_bdot(jnp.where(off, a_ab[p], 0.0), tm[p]) for p in pairs]
        tm = [tm[p] + _bdot(tm[p], t1[p]) for p in pairs]
    akv = [_bdot(a_ak[p], vv[p]).astype(BF16) for p in pairs]
    wu = [_bdot(tm[p], jnp.concatenate([at[p], akv[p]], axis=1)) for p in pairs]
    qy = [jnp.dot(a_rb[p], wu[p].astype(BF16), preferred_element_type=F32) for p in pairs]
    rkv = [_bdot(a_rk[p], vv[p]) for p in pairs]
    ys = []
    for p in pairs:
        s_p = st[p]
        w, u0 = wu[p][:, :W2], wu[p][:, W2:]
        q = rt[p] + qy[p][:, :W2]
        ybd = _nt(q, s_p) + qy[p][:, W2:] + rkv[p]
        ys.append(ybd[:L] + ybd[L:])
        cm = _tn(u0, bh[p]) + _tn(vv[p], kh[p])
        st[p] = s_p * p_last[:, p * W2:(p + 1) * W2] + _bdot(s_p, _tn(w, bh[p])) + cm
    y_ref[...] = jnp.concatenate(ys, axis=1)

    @pl.when(c == nchunks - 1)
    def _():
        so_ref[...] = st[...]


def _wkv_chunked(nbatch, tlen, r, lw, k, v, an, bn, s0):
    nchunks = tlen // CHUNK
    npair, w2 = RWKV_HEADS // 2, 2 * HEAD_DIM
    s5 = s0.reshape(nbatch, npair, 2, HEAD_DIM, HEAD_DIM)
    zero = jnp.zeros_like(s5[:, :, 0])
    s_bd = jnp.concatenate([jnp.concatenate([s5[:, :, 0], zero], axis=-1),
                            jnp.concatenate([zero, s5[:, :, 1]], axis=-1)], axis=-2)
    tok = pl.BlockSpec((CHUNK, RWKV_DIM), lambda b, c: (b * nchunks + c, 0))
    stt = pl.BlockSpec((None, npair, w2, w2), lambda b, c: (b, 0, 0, 0))
    y, so = pl.pallas_call(
        functools.partial(_wkv_chunk_body, nchunks),
        grid=(nbatch, nchunks),
        in_specs=[tok] * 6 + [stt],
        out_specs=[tok, stt],
        out_shape=[jax.ShapeDtypeStruct((nbatch * tlen, RWKV_DIM), F32),
                   jax.ShapeDtypeStruct((nbatch, npair, w2, w2), F32)],
        scratch_shapes=[pltpu.VMEM((npair, w2, w2), F32)],
        compiler_params=_params(2),
        name="wkv_chunked",
    )(r, lw, k, v, an, bn, s_bd)
    s_new = jnp.stack([so[:, :, :HEAD_DIM, :HEAD_DIM], so[:, :, HEAD_DIM:, HEAD_DIM:]], axis=2)
    return y, s_new.reshape(nbatch, RWKV_HEADS, HEAD_DIM, HEAD_DIM)


def _wkv_lane_body(tlen, r_ref, lw_ref, k_ref, v_ref, a_ref, b_ref, s_ref, y_ref, so_ref):
    N = HEAD_DIM
    nb = s_ref.shape[-1]

    def row(ref, t, j):
        return jnp.broadcast_to(ref[t, pl.ds(j, 1), :], (N, nb))

    def first(j, acc):
        return acc + s_ref[j] * row(a_ref, 0, j)

    sa = lax.fori_loop(0, N, first, jnp.zeros((N, nb), F32))
    for t in range(tlen):
        v_t = v_ref[t]
        src = s_ref if t == 0 else so_ref

        def body(j, carry, t=t, v_t=v_t, src=src, sa=sa):
            y, sa_next = carry
            sj = src[j] * jnp.exp(row(lw_ref, t, j)) + sa * row(b_ref, t, j) + v_t * row(k_ref, t, j)
            so_ref[j] = sj
            y = y + sj * row(r_ref, t, j)
            if t + 1 < tlen:
                sa_next = sa_next + sj * row(a_ref, t + 1, j)
            return y, sa_next

        zero = jnp.zeros((N, nb), F32)
        y, sa = lax.fori_loop(0, N, body, (zero, zero))
        y_ref[t] = y


def _wkv_lanes(tlen, r, lw, k, v, an, bn, s_t):
    nb = r.shape[-1]
    tok = pl.BlockSpec((tlen, HEAD_DIM, nb), lambda h: (0, h, 0))
    stt = pl.BlockSpec((None, HEAD_DIM, HEAD_DIM, nb), lambda h: (h, 0, 0, 0))
    return pl.pallas_call(
        functools.partial(_wkv_lane_body, tlen),
        grid=(RWKV_HEADS,),
        in_specs=[tok] * 6 + [stt],
        out_specs=[tok, stt],
        out_shape=[jax.ShapeDtypeStruct((tlen, RWKV_DIM, nb), F32),
                   jax.ShapeDtypeStruct((RWKV_HEADS, HEAD_DIM, HEAD_DIM, nb), F32)],
        compiler_params=_params(1),
        name="wkv_lanes",
    )(r, lw, k, v, an, bn, s_t)


def _conv_body(grp, halo, zc_ref, c0_ref, cw_ref, cb_ref, lg_ref, lb_ref, cv_ref, tail_ref, ext):
    tm, s = grp.tm, grp.stride
    hist = CONV_HIST * s
    j = pl.program_id(0) % grp.tiles_per_seg

    @pl.when(j == 0)
    def _():
        ext[pl.ds(halo - hist, hist), :] = c0_ref[...]

    zc = zc_ref[...]
    u = zc[:, :CONV_DIM] * jax.nn.sigmoid(zc[:, CONV_DIM:])
    ext[pl.ds(halo, tm), :] = u
    acc = jnp.broadcast_to(cb_ref[...], (tm, CONV_DIM))
    for w in range(CONV_WIDTH):
        acc = acc + cw_ref[pl.ds(w, 1), :] * ext[pl.ds(halo - hist + w * s, tm), :]
    y = _layernorm(acc) * lg_ref[...] + lb_ref[...]
    cv_ref[...] = y * jax.nn.sigmoid(y)
    tail_ref[...] = ext[pl.ds(halo + tm - hist, hist), :]
    if grp.tiles_per_seg > 1:
        ext[pl.ds(0, halo), :] = ext[pl.ds(tm, halo), :]


def _conv(grp, zc, conv0, cw, cb, lg, lb):
    hist = CONV_HIST * grp.stride
    halo = _round_up(hist, SUBLANES)
    tps = grp.tiles_per_seg
    hist_spec = pl.BlockSpec((None, hist, CONV_DIM), lambda i: (i // tps, 0, 0))
    return pl.pallas_call(
        functools.partial(_conv_body, grp, halo),
        grid=(grp.tiles,),
        in_specs=[_row_spec(grp, 2 * CONV_DIM), hist_spec,
                  _const_spec(cw.shape), _const_spec(cb.shape), _const_spec(lg.shape), _const_spec(lb.shape)],
        out_specs=[_row_spec(grp, CONV_DIM), hist_spec],
        out_shape=[jax.ShapeDtypeStruct((grp.rows, CONV_DIM), F32),
                   jax.ShapeDtypeStruct((grp.nseg, hist, CONV_DIM), F32)],
        scratch_shapes=[pltpu.VMEM((halo + grp.tm, CONV_DIM), F32)],
        compiler_params=_params(1),
        name="conv_group",
    )(zc, conv0, cw, cb, lg, lb)


def _mix_body(x_ref, y_ref, r_ref, k_ref, v_ref, g_ref, cv_ref, g1_ref, sc2_ref, sh2_ref,
              lnxg_ref, lnxb_ref, rk_ref, wout_ref, l1g_ref, l1b_ref, rw_ref, rb_ref, gmat_ref,
              x1_ref, h2_ref, ti_ref, gt_ref):
    gm = gmat_ref[...]
    y = y_ref[...]
    inv = 1.0 / HEAD_DIM
    ym = _headsum(y, gm) * inv
    yc = y - ym
    yv = _headsum(yc * yc, gm) * inv
    yn = yc * lax.rsqrt(yv + GN_EPS) * lnxg_ref[...] + lnxb_ref[...]
    bonus = _headsum(r_ref[...] * k_ref[...] * rk_ref[...], gm) * v_ref[...]
    yo = (yn + bonus) * g_ref[...]
    mix = (jnp.dot(yo.astype(BF16), wout_ref[pl.ds(0, RWKV_DIM), :], preferred_element_type=F32)
           + jnp.dot(cv_ref[...].astype(BF16), wout_ref[pl.ds(RWKV_DIM, CONV_DIM), :],
                     preferred_element_type=F32))
    xa = DEEPNORM_ALPHA * x_ref[...] + g1_ref[...] * mix
    x1 = _layernorm(xa) * l1g_ref[...] + l1b_ref[...]
    x1_ref[...] = x1
    h2 = x1 * (1.0 + sc2_ref[...]) + sh2_ref[...]
    h2_ref[...] = h2
    logits = jnp.dot(h2, rw_ref[...], precision=lax.Precision.HIGHEST,
                     preferred_element_type=F32) + rb_ref[...]
    lane = lax.broadcasted_iota(jnp.int32, logits.shape, 1).astype(F32)
    cur = logits
    vals, idxs = [], []
    for _ in range(TOP_K):
        m = jnp.max(cur, axis=-1, keepdims=True)
        idx = jnp.min(jnp.where(cur == m, lane, float(LANES)), axis=-1, keepdims=True)
        vals.append(m)
        idxs.append(idx)
        cur = jnp.where(lane == idx, -jnp.inf, cur)
    es = [jnp.exp(vv - vals[0]) for vv in vals]
    den = es[0] + es[1] + es[2] + es[3]
    ti = jnp.zeros_like(logits)
    gt = jnp.zeros_like(logits)
    for kk in range(TOP_K):
        ti = jnp.where(lane == float(kk), idxs[kk], ti)
        gt = jnp.where(lane == float(kk), es[kk] / den, gt)
    ti_ref[...] = ti.astype(jnp.int32)
    gt_ref[...] = gt


def _mix(grp, x, y, r, k, v, g, cv, mod, wts, gmat):
    half = _row_spec(grp, RWKV_DIM)
    names = ["lnx_g", "lnx_b", "r_k", "w_out", "ln1_g", "ln1_b", "router_w", "router_b"]
    in_specs = ([_row_spec(grp, D_MODEL)] + [half] * 6 + [_mod_spec(grp, 2), _mod_spec(grp, 4), _mod_spec(grp, 3)]
                + [_const_spec(wts[nm].shape) for nm in names] + [_const_spec(gmat.shape)])
    args = [x, y, r, k, v, g, cv, mod, mod, mod] + [wts[nm] for nm in names] + [gmat]
    return pl.pallas_call(
        _mix_body,
        grid=(grp.tiles,),
        in_specs=in_specs,
        out_specs=[_row_spec(grp, D_MODEL), _row_spec(grp, D_MODEL), _row_spec(grp, LANES), _row_spec(grp, LANES)],
        out_shape=[jax.ShapeDtypeStruct((grp.rows, D_MODEL), F32),
                   jax.ShapeDtypeStruct((grp.rows, D_MODEL), F32),
                   jax.ShapeDtypeStruct((grp.rows, LANES), jnp.int32),
                   jax.ShapeDtypeStruct((grp.rows, LANES), F32)],
        compiler_params=_params(1),
        name="mix_out",
    )(*args)


def _route_body(ti_ref, pos_ref, cnt_ref, carry):
    i = pl.program_id(0)

    @pl.when(i == 0)
    def _():
        carry[...] = jnp.zeros(carry.shape, F32)

    ti = ti_ref[...]
    tm = ti.shape[0]
    lane = lax.broadcasted_iota(jnp.int32, (tm, LANES), 1)
    sel = [lane == ti[:, kk:kk + 1] for kk in range(TOP_K)]
    onehot = jnp.zeros((tm, LANES), F32)
    for kk in range(TOP_K):
        onehot = onehot + jnp.where(sel[kk], 1.0, 0.0)
    rr = lax.broadcasted_iota(jnp.int32, (tm, tm), 0)
    cc = lax.broadcasted_iota(jnp.int32, (tm, tm), 1)
    before = jnp.where(rr > cc, 1.0, 0.0).astype(BF16)
    cum = jnp.dot(before, onehot.astype(BF16), preferred_element_type=F32) + carry[...]
    pos = jnp.zeros((tm, LANES), F32)
    for kk in range(TOP_K):
        pk = jnp.sum(jnp.where(sel[kk], cum, 0.0), axis=-1, keepdims=True)
        pos = jnp.where(lane == kk, pk, pos)
    pos_ref[...] = pos.astype(jnp.int32)
    carry[...] = carry[...] + jnp.sum(onehot, axis=0, keepdims=True)
    cnt_ref[...] = carry[...]


def _route(top_i, tm):
    n_tok = top_i.shape[0]
    rows = EXPERT_ROWS
    pos, cnt = pl.pallas_call(
        _route_body,
        grid=(n_tok // tm,),
        in_specs=[pl.BlockSpec((tm, LANES), lambda i: (i, 0))],
        out_specs=[pl.BlockSpec((tm, LANES), lambda i: (i, 0)), pl.BlockSpec((1, LANES), lambda i: (0, 0))],
        out_shape=[jax.ShapeDtypeStruct((n_tok, LANES), jnp.int32), jax.ShapeDtypeStruct((1, LANES), F32)],
        scratch_shapes=[pltpu.VMEM((1, LANES), F32)],
        compiler_params=_params(1),
        name="route_rank",
    )(top_i)
    n_blocks = -(-(n_tok * TOP_K) // rows) + N_EXPERTS
    counts = cnt[0, :N_EXPERTS].astype(jnp.int32)
    padded = (counts + rows - 1) // rows * rows
    padded_end = jnp.cumsum(padded)
    padded_start = padded_end - padded
    dest = padded_start[top_i[:, :TOP_K]] + pos[:, :TOP_K]
    block_start = jnp.arange(n_blocks, dtype=jnp.int32) * rows
    block_e = jnp.minimum(jnp.searchsorted(padded_end, block_start, side='right'), N_EXPERTS - 1).astype(jnp.int32)
    n_used = (padded_end[-1] // rows).astype(jnp.int32).reshape(1)
    return dest, block_e, n_used, n_blocks * rows


def _dispatch_body(tm, dst_ref, h_ref, xin_ref, x_hbm, sem):
    del xin_ref

    def body(r, carry):
        for kk in range(TOP_K):
            d = dst_ref[0, r * TOP_K + kk]
            pltpu.make_async_copy(h_ref.at[pl.ds(r, 1), :], x_hbm.at[pl.ds(d, 1), :],
                                  sem.at[kk % 2]).start(priority=kk % 2)
        return carry

    lax.fori_loop(0, tm, body, 0, unroll=8)
    for kk in range(TOP_K):
        pltpu.make_async_copy(h_ref, h_ref, sem.at[kk % 2]).wait()


def _dispatch(grp, h2, dest, x_buf):
    tm = grp.tm
    dblk = dest.reshape(grp.tiles, 1, tm * TOP_K)
    return pl.pallas_call(
        functools.partial(_dispatch_body, tm),
        grid=(grp.tiles,),
        in_specs=[pl.BlockSpec((None, 1, tm * TOP_K), lambda i: (i, 0, 0), memory_space=pltpu.SMEM),
                  _row_spec(grp, D_MODEL), pl.BlockSpec(memory_space=pl.ANY)],
        out_specs=pl.BlockSpec(memory_space=pl.ANY),
        out_shape=jax.ShapeDtypeStruct(x_buf.shape, F32),
        scratch_shapes=[pltpu.SemaphoreType.DMA((2,))],
        input_output_aliases={2: 0},
        compiler_params=pltpu.CompilerParams(dimension_semantics=("arbitrary",), vmem_limit_bytes=VMEM_LIMIT,
                                             has_side_effects=True),
        name="dispatch",
    )(dblk, h2, x_buf)


def _expert_body(be_ref, nu_ref, x_ref, wgu_ref, bgu_ref, wdn_ref, bdn_ref, o_ref, wgu_b, wdn_b):
    i = pl.program_id(0)

    @pl.when(i < nu_ref[0])
    def _():
        prev_e = be_ref[jnp.maximum(i - 1, 0)]

        @pl.when((i == 0) | (be_ref[i] != prev_e))
        def _():
            wgu_b[...] = wgu_ref[...].astype(BF16)
            wdn_b[...] = wdn_ref[...].astype(BF16)

        x = x_ref[...].astype(BF16)
        gu = jnp.dot(x, wgu_b[...], preferred_element_type=F32) + bgu_ref[...]
        gate = jnp.minimum(gu[:, :D_FF], SWIGLU_LIMIT)
        lin = jnp.clip(gu[:, D_FF:], -SWIGLU_LIMIT, SWIGLU_LIMIT)
        act = gate * jax.nn.sigmoid(SWIGLU_ALPHA * gate) * (lin + 1.0)
        o_ref[...] = jnp.dot(act.astype(BF16), wdn_b[...], preferred_element_type=F32) + bdn_ref[...]

    @pl.when(i >= nu_ref[0])
    def _():
        o_ref[...] = jnp.zeros(o_ref.shape, F32)


def _experts(layer, x_buf, block_e, n_used, w_gu, b_gu, w_dn, b_dn):
    n_rows = x_buf.shape[0]
    rows = EXPERT_ROWS
    n_blocks = n_rows // rows

    def blk(i, be, nu):
        return jnp.minimum(i, nu[0] - 1)

    def wmap(i, be, nu):
        return (layer, be[blk(i, be, nu)], 0, 0)

    grid_spec = pltpu.PrefetchScalarGridSpec(
        num_scalar_prefetch=2,
        grid=(n_blocks,),
        in_specs=[pl.BlockSpec((rows, D_MODEL), lambda i, be, nu: (blk(i, be, nu), 0)),
                  pl.BlockSpec((None, None, D_MODEL, 2 * D_FF), wmap),
                  pl.BlockSpec((None, None, 1, 2 * D_FF), wmap),
                  pl.BlockSpec((None, None, D_FF, D_MODEL), wmap),
                  pl.BlockSpec((None, None, 1, D_MODEL), wmap)],
        out_specs=pl.BlockSpec((rows, D_MODEL), lambda i, be, nu: (i, 0)),
        scratch_shapes=[pltpu.VMEM((D_MODEL, 2 * D_FF), BF16), pltpu.VMEM((D_FF, D_MODEL), BF16)])
    return pl.pallas_call(
        _expert_body,
        grid_spec=grid_spec,
        out_shape=jax.ShapeDtypeStruct((n_rows, D_MODEL), F32),
        compiler_params=_params(1),
        name="experts",
    )(block_e, n_used, x_buf, w_gu, b_gu, w_dn, b_dn)


def _combine_body(tm, dcur_ref, dnxt_ref, x1_ref, gt_ref, g2_ref, lg_ref, lb_ref, y_hbm, o_ref, ybuf, sem):
    i = pl.program_id(0)
    slot = i % 2

    def gather(dref, sl):
        def body(r, carry):
            for kk in range(TOP_K):
                d = dref[0, r * TOP_K + kk]
                pltpu.make_async_copy(y_hbm.at[pl.ds(d, 1), :], ybuf.at[sl, kk, pl.ds(r, 1), :],
                                      sem.at[sl, kk % 2]).start(priority=kk % 2)
            return carry
        lax.fori_loop(0, tm, body, 0, unroll=8)

    @pl.when(i == 0)
    def _():
        gather(dcur_ref, 0)

    @pl.when(i + 1 < pl.num_programs(0))
    def _():
        gather(dnxt_ref, 1 - slot)

    for kk in range(TOP_K):
        pltpu.make_async_copy(ybuf.at[slot, kk], ybuf.at[slot, kk], sem.at[slot, kk % 2]).wait()
    gt = gt_ref[...]
    ff = gt[:, 0:1] * ybuf[slot, 0]
    for kk in range(1, TOP_K):
        ff = ff + gt[:, kk:kk + 1] * ybuf[slot, kk]
    xa = DEEPNORM_ALPHA * x1_ref[...] + g2_ref[...] * ff
    o_ref[...] = _layernorm(xa) * lg_ref[...] + lb_ref[...]


def _combine(grp, x1, gates, dest, y_buf, mod, lg, lb):
    tm = grp.tm
    last = grp.tiles - 1
    dblk = dest.reshape(grp.tiles, 1, tm * TOP_K)
    dspec = lambda f: pl.BlockSpec((None, 1, tm * TOP_K), f, memory_space=pltpu.SMEM)
    return pl.pallas_call(
        functools.partial(_combine_body, tm),
        grid=(grp.tiles,),
        in_specs=[dspec(lambda i: (i, 0, 0)), dspec(lambda i: (jnp.minimum(i + 1, last), 0, 0)),
                  _row_spec(grp, D_MODEL), _row_spec(grp, LANES), _mod_spec(grp, 5),
                  _const_spec(lg.shape), _const_spec(lb.shape), pl.BlockSpec(memory_space=pl.ANY)],
        out_specs=_row_spec(grp, D_MODEL),
        out_shape=jax.ShapeDtypeStruct((grp.rows, D_MODEL), F32),
        scratch_shapes=[pltpu.VMEM((2, TOP_K, tm, D_MODEL), F32), pltpu.SemaphoreType.DMA((2, 2))],
        compiler_params=_params(1),
        name="combine_ln",
    )(dblk, dblk, x1, gates, mod, lg, lb, y_buf)


def kernel(x_prompt, x_sample, state_wkv, state_shift, state_conv, c_prompt, c_sample, w_in, tm_mu, w0, w2, a0, a2, g2, v0, v1, v2, k_k, k_a, r_k, lnx_g, lnx_b, conv_w, conv_b, conv_ln_g, conv_ln_b, w_out, ada_w, ada_b, ln1_g, ln1_b, ln2_g, ln2_b, router_w, router_b, w_gu, b_gu, w_dn, b_dn):
    bp, tp, _ = x_prompt.shape
    bs, ts, _ = x_sample.shape
    depth = w_in.shape[0]
    n_p, n_s = bp * tp, bs * ts
    n_tok = n_p + n_s
    gp = _Group(n_p, tp, 1, False)
    gs = _Group(n_s, n_s, bs, True)

    mod_all = _ada(jnp.concatenate([c_prompt, c_sample], axis=0), ada_w, ada_b)

    w_in_b = w_in.astype(BF16)
    w_out_b = w_out.astype(BF16)
    hid = jnp.arange(RWKV_DIM) // HEAD_DIM
    gmat = (hid[:, None] == hid[None, :]).astype(BF16)
    zero_wa = jnp.zeros((depth, DECAY_LORA, RWKV_DIM), F32)
    wwa = jnp.concatenate([jnp.concatenate([w2, zero_wa], axis=2),
                           jnp.concatenate([zero_wa, a2], axis=2)], axis=1).astype(BF16)
    v1p = jnp.pad(v1, ((0, 0), (0, 0), (0, LANES - VRES_LORA))).astype(BF16)
    v2p = jnp.pad(v2, ((0, 0), (0, LANES - VRES_LORA), (0, 0))).astype(BF16)
    rw_p = jnp.pad(router_w, ((0, 0), (0, 0), (0, LANES - N_EXPERTS)))
    rb_p = jnp.pad(router_b, ((0, 0), (0, LANES - N_EXPERTS)), constant_values=-1e30)
    b_gu4 = b_gu.reshape(depth, N_EXPERTS, 1, 2 * D_FF)
    b_dn4 = b_dn.reshape(depth, N_EXPERTS, 1, D_MODEL)

    def row2(a):
        return a.reshape(1, -1)

    x_p = x_prompt.reshape(n_p, D_MODEL)
    x_s = jnp.transpose(x_sample, (1, 0, 2)).reshape(n_s, D_MODEL)
    zeros_wkv = jnp.zeros((bp, RWKV_HEADS, HEAD_DIM, HEAD_DIM), F32)
    zeros_shift = jnp.zeros((bp, 1, SHIFT_DIM), F32)
    zeros_conv = jnp.zeros((bp, CONV_HIST, CONV_DIM), F32)
    wkv_s_t = jnp.transpose(state_wkv, (0, 2, 4, 3, 1))

    vf_p = vf_s = None
    wkv_p_out, wkv_s_out, shift_p_out, shift_s_out, conv_p_out, conv_s_out = [], [], [], [], [], []
    for l in range(depth):
        mod_p = mod_all[l, :bp].reshape(bp, 1, 6 * D_MODEL)
        mod_s = jnp.tile(mod_all[l, bp:], (ts, 1))
        wts = {"mu": row2(tm_mu[l]), "w0": row2(w0[l]), "a0": row2(a0[l]), "k_k": row2(k_k[l]),
               "k_a": row2(k_a[l]), "wwa": wwa[l], "g2": g2[l].astype(BF16),
               "lnx_g": row2(lnx_g[l]), "lnx_b": row2(lnx_b[l]), "r_k": row2(r_k[l]), "w_out": w_out_b[l],
               "ln1_g": row2(ln1_g[l]), "ln1_b": row2(ln1_b[l]), "router_w": rw_p[l], "router_b": row2(rb_p[l])}
        if l > 0:
            wts.update({"v0": row2(v0[l - 1]), "v1": v1p[l - 1], "v2": v2p[l - 1]})

        per_group = []
        for grp, x, mod, is_p in ((gp, x_p, mod_p, True), (gs, x_s, mod_s, False)):
            zs, zc = _inproj(grp, x, mod, w_in_b[l])
            if is_p:
                shift0, conv0 = zeros_shift, zeros_conv
            else:
                shift0 = state_shift[l][None]
                conv0 = jnp.transpose(state_conv[l], (1, 0, 2)).reshape(1, CONV_HIST * bs, CONV_DIM)
            vfirst = vf_p if is_p else vf_s
            r, lw, k, v, an, bn, g = _prep(grp, zs, shift0, vfirst if l > 0 else None, wts, gmat)
            if l == 0:
                if is_p:
                    vf_p = v
                else:
                    vf_s = v
            if is_p:
                y, s_new = _wkv_chunked(bp, tp, r, lw, k, v, an, bn, zeros_wkv)
                wkv_p_out.append(s_new)
                shift_p_out.append(zs.reshape(bp, tp, SHIFT_DIM)[:, -1])
            else:
                tmaj = lambda a: jnp.transpose(a.reshape(ts, bs, RWKV_DIM), (0, 2, 1))
                y_t, s_new = _wkv_lanes(ts, tmaj(r), tmaj(lw), tmaj(k), tmaj(v), tmaj(an), tmaj(bn), wkv_s_t[l])
                y = jnp.transpose(y_t, (0, 2, 1)).reshape(n_s, RWKV_DIM)
                wkv_s_out.append(jnp.transpose(s_new, (3, 0, 2, 1)))
                shift_s_out.append(zs[(ts - 1) * bs:])
            cv, tail = _conv(grp, zc, conv0, conv_w[l], row2(conv_b[l]), row2(conv_ln_g[l]), row2(conv_ln_b[l]))
            if is_p:
                conv_p_out.append(tail)
            else:
                conv_s_out.append(jnp.transpose(tail.reshape(CONV_HIST, bs, CONV_DIM), (1, 0, 2)))
            off = 0 if is_p else n_p
            x1, h2, ti, gt = _mix(grp, x, y, r, k, v, g, cv, mod, wts, gmat)
            per_group.append((grp, x1, h2, ti, gt, mod, off))

        top_i = jnp.concatenate([pg[3] for pg in per_group], axis=0)
        dest, block_e, n_used, n_rows = _route(top_i, gp.tm)
        if l == 0:
            x_buf = jnp.zeros((n_rows, D_MODEL), F32)
        for grp, x1, h2, ti, gt, mod, off in per_group:
            x_buf = _dispatch(grp, h2, dest[off:off + grp.rows], x_buf)
        y_buf = _experts(l, x_buf, block_e, n_used, w_gu, b_gu4, w_dn, b_dn4)
        outs = []
        for grp, x1, h2, ti, gt, mod, off in per_group:
            outs.append(_combine(grp, x1, gt, dest[off:off + grp.rows], y_buf, mod, row2(ln2_g[l]), row2(ln2_b[l])))
        x_p, x_s = outs

    y_prompt = x_p.reshape(bp, tp, D_MODEL)
    y_sample = jnp.transpose(x_s.reshape(ts, bs, D_MODEL), (1, 0, 2))
    return (y_prompt, y_sample, jnp.stack(wkv_p_out), jnp.stack(wkv_s_out), jnp.stack(shift_p_out),
            jnp.stack(shift_s_out), jnp.stack(conv_p_out), jnp.stack(conv_s_out))
```

```python
import functools

import jax
import jax.numpy as jnp
from jax import lax
from jax.experimental import pallas as pl
from jax.experimental.pallas import tpu as pltpu

F32 = jnp.float32
BF16 = jnp.bfloat16

D_MODEL = 1024
DEPTH = 4
HEAD_DIM = 64
RWKV_DIM = 512
RWKV_HEADS = RWKV_DIM // HEAD_DIM
CONV_DIM = 512
CONV_WIDTH = 31
CONV_HIST = CONV_WIDTH - 1
DECAY_LORA = 64
AAA_LORA = 64
GATE_LORA = 128
VRES_LORA = 32
SHIFT_DIM = 3 * RWKV_DIM + DECAY_LORA + AAA_LORA + GATE_LORA
PROJ_DIM = SHIFT_DIM + 2 * CONV_DIM
N_EXPERTS = 32
TOP_K = 4
D_FF = D_MODEL
SWIGLU_LIMIT = 7.0
SWIGLU_ALPHA = 1.702
LN_EPS = 1e-5
GN_EPS = 64e-5
DEEPNORM_ALPHA = (2 * DEPTH) ** 0.25

LANES = 128
SUBLANES = 8
ROW_TILE = 512
CHUNK = 64
WKV_BATCHES_PER_STEP = 2
EXPERT_ROWS = 256
VMEM_LIMIT = 56 * 1024 * 1024


def _params(n_axes=1, vmem=VMEM_LIMIT):
    return pltpu.CompilerParams(dimension_semantics=("arbitrary",) * n_axes, vmem_limit_bytes=vmem)


def _round_up(x, m):
    return (x + m - 1) // m * m


def _bdot(a, b):
    return jnp.dot(a.astype(BF16), b.astype(BF16), preferred_element_type=F32)


def _split3(x):
    hi = x.astype(BF16)
    r1 = x - hi.astype(F32)
    mid = r1.astype(BF16)
    lo = (r1 - mid.astype(F32)).astype(BF16)
    return hi, mid, lo


def _headsum(x, g):
    return jnp.dot(x.astype(BF16), g, preferred_element_type=F32)


def _layernorm(x):
    mu = jnp.mean(x, axis=-1, keepdims=True)
    xc = x - mu
    var = jnp.mean(xc * xc, axis=-1, keepdims=True)
    return xc * lax.rsqrt(var + LN_EPS)


def _ada_body(c_ref, w_ref, b_ref, o_ref):
    c = c_ref[...]
    cs = c * jax.nn.sigmoid(c)
    o_ref[...] = _bdot(cs, w_ref[...]) + b_ref[...]


def _ada(c_all, ada_w, ada_b):
    depth, _, width = ada_w.shape
    nb = c_all.shape[0]
    tn = 1536
    return pl.pallas_call(
        _ada_body,
        grid=(depth, width // tn),
        in_specs=[pl.BlockSpec((nb, D_MODEL), lambda l, j: (0, 0)),
                  pl.BlockSpec((None, D_MODEL, tn), lambda l, j: (l, 0, j)),
                  pl.BlockSpec((None, 1, tn), lambda l, j: (l, 0, j))],
        out_specs=pl.BlockSpec((None, nb, tn), lambda l, j: (l, 0, j)),
        out_shape=jax.ShapeDtypeStruct((depth, nb, width), F32),
        compiler_params=_params(2),
        name="ada_mod",
    )(c_all, ada_w, ada_b.reshape(depth, 1, width))


class _Group:
    def __init__(self, rows, seg_len, stride, mod_per_row):
        self.rows = rows
        self.seg_len = seg_len
        self.stride = stride
        self.mod_per_row = mod_per_row
        self.tm = min(ROW_TILE, rows)
        assert rows % self.tm == 0 and seg_len % self.tm == 0
        self.tiles = rows // self.tm
        self.tiles_per_seg = seg_len // self.tm
        self.nseg = rows // seg_len


def _mod_spec(grp, which):
    if grp.mod_per_row:
        return pl.BlockSpec((grp.tm, D_MODEL), lambda i: (i, which))
    tps = grp.tiles_per_seg
    return pl.BlockSpec((None, 1, D_MODEL), lambda i: (i // tps, 0, which))


def _row_spec(grp, width):
    return pl.BlockSpec((grp.tm, width), lambda i: (i, 0))


def _const_spec(shape):
    nd = len(shape)
    return pl.BlockSpec(shape, lambda i: (0,) * nd)


def _inproj_body(x_ref, sc_ref, sh_ref, w_ref, zs_ref, zc_ref):
    h = x_ref[...] * (1.0 + sc_ref[...]) + sh_ref[...]
    z = jnp.dot(h.astype(BF16), w_ref[...], preferred_element_type=F32)
    zs_ref[...] = z[:, :SHIFT_DIM]
    zc_ref[...] = z[:, SHIFT_DIM:]


def _inproj(grp, x, mod, w_in_b):
    return pl.pallas_call(
        _inproj_body,
        grid=(grp.tiles,),
        in_specs=[_row_spec(grp, D_MODEL), _mod_spec(grp, 1), _mod_spec(grp, 0),
                  _const_spec((D_MODEL, PROJ_DIM))],
        out_specs=[_row_spec(grp, SHIFT_DIM), _row_spec(grp, 2 * CONV_DIM)],
        out_shape=[jax.ShapeDtypeStruct((grp.rows, SHIFT_DIM), F32),
                   jax.ShapeDtypeStruct((grp.rows, 2 * CONV_DIM), F32)],
        compiler_params=_params(1),
        name="inproj",
    )(x, mod, mod, w_in_b)


def _prep_body(has_vres, grp, halo, *refs):
    it = iter(refs)
    zs_ref, sh0_ref = next(it), next(it)
    vf_ref = next(it) if has_vres else None
    mu_ref, w0_ref, a0_ref, kk_ref, ka_ref, wwa_ref, g2_ref = (next(it) for _ in range(7))
    if has_vres:
        v0_ref, v1_ref, v2_ref = next(it), next(it), next(it)
    gmat_ref = next(it)
    r_o, lw_o, k_o, v_o, an_o, bn_o, g_o = (next(it) for _ in range(7))
    ext = next(it)

    tm, s = grp.tm, grp.stride
    j = pl.program_id(0) % grp.tiles_per_seg

    @pl.when(j == 0)
    def _():
        ext[pl.ds(halo - s, s), :] = sh0_ref[...]

    zs = zs_ref[...]
    ext[pl.ds(halo, tm), :] = zs
    prev = ext[pl.ds(halo - s, tm), :]
    if grp.tiles_per_seg > 1:
        ext[pl.ds(0, halo), :] = ext[pl.ds(tm, halo), :]

    zm = zs + (prev - zs) * mu_ref[...]
    i1, i2, i3 = RWKV_DIM, 2 * RWKV_DIM, 3 * RWKV_DIM
    i5 = i3 + DECAY_LORA + AAA_LORA
    r, k, v = zm[:, :i1], zm[:, i1:i2], zm[:, i2:i3]
    zwa, zg = zm[:, i3:i5], zm[:, i5:]
    lane = lax.broadcasted_iota(jnp.int32, zwa.shape, 1)
    xwa = jnp.where(lane < DECAY_LORA, jnp.tanh(zwa), zwa)
    wa = _bdot(xwa, wwa_ref[...])
    w_in = w0_ref[...] + wa[:, :RWKV_DIM]
    neg = -w_in
    softplus = jnp.maximum(neg, 0.0) + jnp.log(1.0 + jnp.exp(-jnp.abs(neg)))
    w_log = -softplus - 0.5
    lw = -jnp.exp(w_log)
    a = jax.nn.sigmoid(a0_ref[...] + wa[:, RWKV_DIM:])
    g = _bdot(jax.nn.sigmoid(zg), g2_ref[...])
    if has_vres:
        vl = _bdot(_bdot(v, v1_ref[...]), v2_ref[...])
        vg = jax.nn.sigmoid(v0_ref[...] + vl)
        v = v + (vf_ref[...] - v) * vg
    kk = k * kk_ref[...]
    ss = _headsum(kk * kk, gmat_ref[...])
    kk = kk * lax.rsqrt(jnp.maximum(ss, 1e-24))
    k = k * (1.0 + (a - 1.0) * ka_ref[...])
    r_o[...] = r
    lw_o[...] = lw
    k_o[...] = k
    v_o[...] = v
    an_o[...] = -kk
    bn_o[...] = kk * a
    g_o[...] = g


def _prep(grp, zs, shift0, vfirst, wts, gmat):
    has_vres = vfirst is not None
    halo = _round_up(grp.stride, SUBLANES)
    s = grp.stride
    tps = grp.tiles_per_seg
    in_specs = [_row_spec(grp, SHIFT_DIM),
                pl.BlockSpec((None, s, SHIFT_DIM), lambda i: (i // tps, 0, 0))]
    args = [zs, shift0]
    if has_vres:
        in_specs.append(_row_spec(grp, RWKV_DIM))
        args.append(vfirst)
    names = ["mu", "w0", "a0", "k_k", "k_a", "wwa", "g2"] + (["v0", "v1", "v2"] if has_vres else [])
    for nm in names:
        in_specs.append(_const_spec(wts[nm].shape))
        args.append(wts[nm])
    in_specs.append(_const_spec(gmat.shape))
    args.append(gmat)
    out = jax.ShapeDtypeStruct((grp.rows, RWKV_DIM), F32)
    return pl.pallas_call(
        functools.partial(_prep_body, has_vres, grp, halo),
        grid=(grp.tiles,),
        in_specs=in_specs,
        out_specs=[_row_spec(grp, RWKV_DIM)] * 7,
        out_shape=[out] * 7,
        scratch_shapes=[pltpu.VMEM((halo + grp.tm, SHIFT_DIM), F32)],
        compiler_params=_params(1),
        name="rwkv_prep",
    )(*args)


def _nt(a, b):
    return lax.dot_general(a.astype(BF16), b.astype(BF16), (((1,), (1,)), ((), ())),
                           preferred_element_type=F32)


def _tn(a, b):
    return lax.dot_general(a.astype(BF16), b.astype(BF16), (((0,), (0,)), ((), ())),
                           preferred_element_type=F32)


def _wkv_chunk_body(nchunks, nb, r_ref, lw_ref, k_ref, v_ref, a_ref, b_ref, s0_ref, y_ref, so_ref, st):
    c = pl.program_id(1)
    L, N = CHUNK, HEAD_DIM

    @pl.when(c == 0)
    def _():
        st[...] = s0_ref[...]

    P2, W2 = 2 * L, 2 * N
    assert L == N and W2 == LANES
    trow = lax.broadcasted_iota(jnp.int32, (L, L), 0)
    tcol = lax.broadcasted_iota(jnp.int32, (L, L), 1)
    tri = jnp.where(trow >= tcol, 1.0, 0.0).astype(BF16)
    row = lax.broadcasted_iota(jnp.int32, (P2, P2), 0)
    col = lax.broadcasted_iota(jnp.int32, (P2, P2), 1)
    lg = L.bit_length() - 1
    same_head = (row >> lg) == (col >> lg)
    strict = same_head & (row > col)
    incl = same_head & (row >= col)
    eye = row == col
    sh = 3
    diag_blk = (row >> sh) == (col >> sh)
    off_masks = []
    while (1 << sh) < L:
        off_masks.append(((row >> sh) == (col >> sh) + 1) & ((row >> (sh + 1)) == (col >> (sh + 1))))
        sh += 1

    lane_lo = lax.broadcasted_iota(jnp.int32, (L, W2), 1) < N
    npair = RWKV_HEADS // 2

    def stack(x, p):
        xp = x[:, p * W2:(p + 1) * W2]
        return jnp.concatenate([jnp.where(lane_lo, xp, 0.0), jnp.where(lane_lo, 0.0, xp)], axis=0)

    at, rt, bt, kt, bh, kh, vv, p_last = [], [], [], [], [], [], [], []
    for bi in range(nb):
        lw = lw_ref[bi]
        h3 = _split3(lw)
        cl = (jnp.dot(tri, h3[0], preferred_element_type=F32) + jnp.dot(tri, h3[1], preferred_element_type=F32)
              + jnp.dot(tri, h3[2], preferred_element_type=F32))
        cl_last = cl[L - 1:L, :]
        ecl = jnp.exp(cl)
        encl = jnp.exp(-cl)
        dl = jnp.exp(cl_last - cl)
        pl_row = jnp.exp(cl_last)
        an, bn, kx, rx, vx = a_ref[bi], b_ref[bi], k_ref[bi], r_ref[bi], v_ref[bi]
        at_all, rt_all = an * jnp.exp(cl - lw), rx * ecl
        bt_all, kt_all, bh_all, kh_all = bn * encl, kx * encl, bn * dl, kx * dl
        for p in range(npair):
            at.append(stack(at_all, p).astype(BF16))
            rt.append(stack(rt_all, p))
            bt.append(stack(bt_all, p).astype(BF16))
            kt.append(stack(kt_all, p).astype(BF16))
            bh.append(stack(bh_all, p).astype(BF16))
            kh.append(stack(kh_all, p).astype(BF16))
            vv.append(stack(vx, p).astype(BF16))
            p_last.append(pl_row[:, p * W2:(p + 1) * W2])
    pairs = range(nb * npair)

    ar = [jnp.concatenate([at[p], rt[p].astype(BF16)], axis=0) for p in pairs]
    xb = [_nt(ar[p], bt[p]) for p in pairs]
    xk = [_nt(ar[p], kt[p]) for p in pairs]
    a_ab = [jnp.where(strict, xb[p][:P2], 0.0) for p in pairs]
    a_ak = [jnp.where(strict, xk[p][:P2], 0.0) for p in pairs]
    a_rb = [jnp.where(incl, xb[p][P2:], 0.0).astype(BF16) for p in pairs]
    a_rk = [jnp.where(incl, xk[p][P2:], 0.0) for p in pairs]
    pw = [jnp.where(diag_blk, a_ab[p], 0.0) for p in pairs]
    tm = [jnp.where(eye, 1.0, 0.0) + pw[p] for p in pairs]
    for _ in range(2):
        pw = [_bdot(pw[p], pw[p]) for p in pairs]
        tm = [tm[p] + _bdot(tm[p], pw[p]) for p in pairs]
    for off in off_masks:
        t1 = [_bdot(jnp.where(off, a_ab[p], 0.0), tm[p]) for p in pairs]
        tm = [tm[p] + _bdot(tm[p], t1[p]) for p in pairs]
    akv = [_bdot(a_ak[p], vv[p]).astype(BF16) for p in pairs]
    wu = [_bdot(tm[p], jnp.concatenate([at[p], akv[p]], axis=1)) for p in pairs]
    qy = [jnp.dot(a_rb[p], wu[p].astype(BF16), preferred_element_type=F32) for p in pairs]
    rkv = [_bdot(a_rk[p], vv[p]) for p in pairs]
    ys = []
    for p in pairs:
        bi, pp = divmod(p, npair)
        s_p = st[bi, pp]
        w, u0 = wu[p][:, :W2], wu[p][:, W2:]
        q = rt[p] + qy[p][:, :W2]
        ybd = _nt(q, s_p) + qy[p][:, W2:] + rkv[p]
        ys.append(ybd[:L] + ybd[L:])
        cm = _tn(u0, bh[p]) + _tn(vv[p], kh[p])
        st[bi, pp] = s_p * p_last[p] + _bdot(s_p, _tn(w, bh[p])) + cm
    for bi in range(nb):
        y_ref[bi] = jnp.concatenate(ys[bi * npair:(bi + 1) * npair], axis=1)

    @pl.when(c == nchunks - 1)
    def _():
        so_ref[...] = st[...]


def _wkv_chunked(nbatch, tlen, r, lw, k, v, an, bn, s0):
    nchunks = tlen // CHUNK
    npair, w2 = RWKV_HEADS // 2, 2 * HEAD_DIM
    s5 = s0.reshape(nbatch, npair, 2, HEAD_DIM, HEAD_DIM)
    zero = jnp.zeros_like(s5[:, :, 0])
    s_bd = jnp.concatenate([jnp.concatenate([s5[:, :, 0], zero], axis=-1),
                            jnp.concatenate([zero, s5[:, :, 1]], axis=-1)], axis=-2)
    nb = WKV_BATCHES_PER_STEP if nbatch % WKV_BATCHES_PER_STEP == 0 else 1
    tok = pl.BlockSpec((nb, CHUNK, RWKV_DIM), lambda b, c: (b, c, 0))
    stt = pl.BlockSpec((nb, npair, w2, w2), lambda b, c: (b, 0, 0, 0))
    seq = lambda t: t.reshape(nbatch, tlen, RWKV_DIM)
    y, so = pl.pallas_call(
        functools.partial(_wkv_chunk_body, nchunks, nb),
        grid=(nbatch // nb, nchunks),
        in_specs=[tok] * 6 + [stt],
        out_specs=[tok, stt],
        out_shape=[jax.ShapeDtypeStruct((nbatch, tlen, RWKV_DIM), F32),
                   jax.ShapeDtypeStruct((nbatch, npair, w2, w2), F32)],
        scratch_shapes=[pltpu.VMEM((nb, npair, w2, w2), F32)],
        compiler_params=_params(2),
        name="wkv_chunked",
    )(seq(r), seq(lw), seq(k), seq(v), seq(an), seq(bn), s_bd)
    s_new = jnp.stack([so[:, :, :HEAD_DIM, :HEAD_DIM], so[:, :, HEAD_DIM:, HEAD_DIM:]], axis=2)
    return y.reshape(nbatch * tlen, RWKV_DIM), s_new.reshape(nbatch, RWKV_HEADS, HEAD_DIM, HEAD_DIM)


def _wkv_lane_body(tlen, r_ref, lw_ref, k_ref, v_ref, a_ref, b_ref, s_ref, y_ref, so_ref):
    N = HEAD_DIM
    nb = s_ref.shape[-1]

    def row(ref, t, j):
        return jnp.broadcast_to(ref[t, pl.ds(j, 1), :], (N, nb))

    def first(j, acc):
        return acc + s_ref[j] * row(a_ref, 0, j)

    sa = lax.fori_loop(0, N, first, jnp.zeros((N, nb), F32))
    for t in range(tlen):
        v_t = v_ref[t]
        src = s_ref if t == 0 else so_ref

        def body(j, carry, t=t, v_t=v_t, src=src, sa=sa):
            y, sa_next = carry
            sj = src[j] * jnp.exp(row(lw_ref, t, j)) + sa * row(b_ref, t, j) + v_t * row(k_ref, t, j)
            so_ref[j] = sj
            y = y + sj * row(r_ref, t, j)
            if t + 1 < tlen:
                sa_next = sa_next + sj * row(a_ref, t + 1, j)
            return y, sa_next

        zero = jnp.zeros((N, nb), F32)
        y, sa = lax.fori_loop(0, N, body, (zero, zero))
        y_ref[t] = y


def _wkv_lanes(tlen, r, lw, k, v, an, bn, s_t):
    nb = r.shape[-1]
    tok = pl.BlockSpec((tlen, HEAD_DIM, nb), lambda h: (0, h, 0))
    stt = pl.BlockSpec((None, HEAD_DIM, HEAD_DIM, nb), lambda h: (h, 0, 0, 0))
    return pl.pallas_call(
        functools.partial(_wkv_lane_body, tlen),
        grid=(RWKV_HEADS,),
        in_specs=[tok] * 6 + [stt],
        out_specs=[tok, stt],
        out_shape=[jax.ShapeDtypeStruct((tlen, RWKV_DIM, nb), F32),
                   jax.ShapeDtypeStruct((RWKV_HEADS, HEAD_DIM, HEAD_DIM, nb), F32)],
        compiler_params=_params(1),
        name="wkv_lanes",
    )(r, lw, k, v, an, bn, s_t)


def _conv_body(grp, halo, zc_ref, c0_ref, cw_ref, cb_ref, lg_ref, lb_ref, cv_ref, tail_ref, ext, win):
    tm, s = grp.tm, grp.stride
    hist = CONV_HIST * s
    j = pl.program_id(0) % grp.tiles_per_seg

    @pl.when(j == 0)
    def _():
        ext[pl.ds(halo - hist, hist), :] = c0_ref[...]

    zc = zc_ref[...]
    u = zc[:, :CONV_DIM] * jax.nn.sigmoid(zc[:, CONV_DIM:])
    ext[pl.ds(halo, tm), :] = u
    acc = jnp.broadcast_to(cb_ref[...], (tm, CONV_DIM))
    offs = [halo - hist + w * s for w in range(CONV_WIDTH)]
    for phase in range(SUBLANES):
        taps = [w for w in range(CONV_WIDTH) if offs[w] % SUBLANES == phase]
        if not taps:
            continue
        if s % SUBLANES == 0:
            for w in taps:
                acc = acc + cw_ref[pl.ds(w, 1), :] * ext[pl.ds(offs[w], tm), :]
            continue
        lo = offs[taps[0]]
        span = tm + offs[taps[-1]] - lo
        win[pl.ds(0, span), :] = ext[pl.ds(lo, span), :]
        for w in taps:
            acc = acc + cw_ref[pl.ds(w, 1), :] * win[pl.ds(offs[w] - lo, tm), :]
    y = _layernorm(acc) * lg_ref[...] + lb_ref[...]
    cv_ref[...] = y * jax.nn.sigmoid(y)
    tail_ref[...] = ext[pl.ds(halo + tm - hist, hist), :]
    if grp.tiles_per_seg > 1:
        ext[pl.ds(0, halo), :] = ext[pl.ds(tm, halo), :]


def _conv(grp, zc, conv0, cw, cb, lg, lb):
    hist = CONV_HIST * grp.stride
    halo = _round_up(hist, SUBLANES)
    tps = grp.tiles_per_seg
    hist_spec = pl.BlockSpec((None, hist, CONV_DIM), lambda i: (i // tps, 0, 0))
    return pl.pallas_call(
        functools.partial(_conv_body, grp, halo),
        grid=(grp.tiles,),
        in_specs=[_row_spec(grp, 2 * CONV_DIM), hist_spec,
                  _const_spec(cw.shape), _const_spec(cb.shape), _const_spec(lg.shape), _const_spec(lb.shape)],
        out_specs=[_row_spec(grp, CONV_DIM), hist_spec],
        out_shape=[jax.ShapeDtypeStruct((grp.rows, CONV_DIM), F32),
                   jax.ShapeDtypeStruct((grp.nseg, hist, CONV_DIM), F32)],
        scratch_shapes=[pltpu.VMEM((halo + grp.tm, CONV_DIM), F32),
                        pltpu.VMEM((grp.tm + _round_up(CONV_HIST, SUBLANES), CONV_DIM), F32)],
        compiler_params=_params(1),
        name="conv_group",
    )(zc, conv0, cw, cb, lg, lb)


def _mix_body(x_ref, y_ref, r_ref, k_ref, v_ref, g_ref, cv_ref, g1_ref, sc2_ref, sh2_ref,
              lnxg_ref, lnxb_ref, rk_ref, wout_ref, l1g_ref, l1b_ref, rw_ref, rwl_ref, rb_ref, gmat_ref,
              x1_ref, h2_ref, ti_ref, gt_ref):
    gm = gmat_ref[...]
    y = y_ref[...]
    inv = 1.0 / HEAD_DIM
    ym = _headsum(y, gm) * inv
    yc = y - ym
    yv = _headsum(yc * yc, gm) * inv
    yn = yc * lax.rsqrt(yv + GN_EPS) * lnxg_ref[...] + lnxb_ref[...]
    bonus = _headsum(r_ref[...] * k_ref[...] * rk_ref[...], gm) * v_ref[...]
    yo = (yn + bonus) * g_ref[...]
    mix = (jnp.dot(yo.astype(BF16), wout_ref[pl.ds(0, RWKV_DIM), :], preferred_element_type=F32)
           + jnp.dot(cv_ref[...].astype(BF16), wout_ref[pl.ds(RWKV_DIM, CONV_DIM), :],
                     preferred_element_type=F32))
    xa = DEEPNORM_ALPHA * x_ref[...] + g1_ref[...] * mix
    x1 = _layernorm(xa) * l1g_ref[...] + l1b_ref[...]
    x1_ref[...] = x1
    h2 = x1 * (1.0 + sc2_ref[...]) + sh2_ref[...]
    h2_ref[...] = h2
    h_hi = h2.astype(BF16)
    h_lo = (h2 - h_hi.astype(F32)).astype(BF16)
    logits = (jnp.dot(h_hi, rw_ref[...], preferred_element_type=F32)
              + jnp.dot(h_hi, rwl_ref[...], preferred_element_type=F32)
              + jnp.dot(h_lo, rw_ref[...], preferred_element_type=F32)) + rb_ref[...]
    lane = lax.broadcasted_iota(jnp.int32, logits.shape, 1).astype(F32)
    cur = logits
    vals, idxs = [], []
    for _ in range(TOP_K):
        m = jnp.max(cur, axis=-1, keepdims=True)
        idx = jnp.min(jnp.where(cur == m, lane, float(LANES)), axis=-1, keepdims=True)
        vals.append(m)
        idxs.append(idx)
        cur = jnp.where(lane == idx, -jnp.inf, cur)
    es = [jnp.exp(vv - vals[0]) for vv in vals]
    den = es[0] + es[1] + es[2] + es[3]
    ti = jnp.zeros_like(logits)
    gt = jnp.zeros_like(logits)
    for kk in range(TOP_K):
        ti = jnp.where(lane == float(kk), idxs[kk], ti)
        gt = jnp.where(lane == float(kk), es[kk] / den, gt)
    ti_ref[...] = ti.astype(jnp.int32)
    gt_ref[...] = gt


def _mix(grp, x, y, r, k, v, g, cv, mod, wts, gmat):
    half = _row_spec(grp, RWKV_DIM)
    names = ["lnx_g", "lnx_b", "r_k", "w_out", "ln1_g", "ln1_b", "router_w", "router_w_lo", "router_b"]
    in_specs = ([_row_spec(grp, D_MODEL)] + [half] * 6 + [_mod_spec(grp, 2), _mod_spec(grp, 4), _mod_spec(grp, 3)]
                + [_const_spec(wts[nm].shape) for nm in names] + [_const_spec(gmat.shape)])
    args = [x, y, r, k, v, g, cv, mod, mod, mod] + [wts[nm] for nm in names] + [gmat]
    return pl.pallas_call(
        _mix_body,
        grid=(grp.tiles,),
        in_specs=in_specs,
        out_specs=[_row_spec(grp, D_MODEL), _row_spec(grp, D_MODEL), _row_spec(grp, LANES), _row_spec(grp, LANES)],
        out_shape=[jax.ShapeDtypeStruct((grp.rows, D_MODEL), F32),
                   jax.ShapeDtypeStruct((grp.rows, D_MODEL), F32),
                   jax.ShapeDtypeStruct((grp.rows, LANES), jnp.int32),
                   jax.ShapeDtypeStruct((grp.rows, LANES), F32)],
        compiler_params=_params(1),
        name="mix_out",
    )(*args)


def _route_body(ti_ref, dest_ref, cnt_ref, carry, pstart):
    phase = pl.program_id(0)
    i = pl.program_id(1)
    ti = ti_ref[...]
    tm = ti.shape[0]
    lane = lax.broadcasted_iota(jnp.int32, (tm, LANES), 1)
    sel = [lane == ti[:, kk:kk + 1] for kk in range(TOP_K)]
    onehot = jnp.zeros((tm, LANES), F32)
    for kk in range(TOP_K):
        onehot = onehot + jnp.where(sel[kk], 1.0, 0.0)

    @pl.when((phase == 0) & (i == 0))
    def _():
        carry[...] = jnp.zeros(carry.shape, F32)

    @pl.when((phase == 1) & (i == 0))
    def _():
        cnt = carry[...]
        cnt_ref[...] = cnt
        padded = jnp.floor((cnt + (EXPERT_ROWS - 1)) * (1.0 / EXPERT_ROWS)) * EXPERT_ROWS
        er = lax.broadcasted_iota(jnp.int32, (LANES, LANES), 0)
        ec = lax.broadcasted_iota(jnp.int32, (LANES, LANES), 1)
        earlier = jnp.where(er < ec, 1.0, 0.0).astype(BF16)
        h3 = _split3(jnp.broadcast_to(padded, (SUBLANES, LANES)))
        acc = (jnp.dot(h3[0], earlier, preferred_element_type=F32)
               + jnp.dot(h3[1], earlier, preferred_element_type=F32)
               + jnp.dot(h3[2], earlier, preferred_element_type=F32))
        pstart[...] = acc[0:1, :]
        carry[...] = jnp.zeros(carry.shape, F32)

    @pl.when(phase == 1)
    def _():
        rr = lax.broadcasted_iota(jnp.int32, (tm, tm), 0)
        cc = lax.broadcasted_iota(jnp.int32, (tm, tm), 1)
        before = jnp.where(rr > cc, 1.0, 0.0).astype(BF16)
        cum = jnp.dot(before, onehot.astype(BF16), preferred_element_type=F32) + (carry[...] + pstart[...])
        dest = jnp.zeros((tm, LANES), F32)
        for kk in range(TOP_K):
            pk = jnp.sum(jnp.where(sel[kk], cum, 0.0), axis=-1, keepdims=True)
            dest = jnp.where(lane == kk, pk, dest)
        dest_ref[...] = dest.astype(jnp.int32)

    carry[...] = carry[...] + jnp.sum(onehot, axis=0, keepdims=True)


def _route(top_i, tm):
    n_tok = top_i.shape[0]
    rows = EXPERT_ROWS
    dest, cnt = pl.pallas_call(
        _route_body,
        grid=(2, n_tok // tm),
        in_specs=[pl.BlockSpec((tm, LANES), lambda ph, i: (i, 0))],
        out_specs=[pl.BlockSpec((tm, LANES), lambda ph, i: (i * ph, 0)),
                   pl.BlockSpec((1, LANES), lambda ph, i: (0, 0))],
        out_shape=[jax.ShapeDtypeStruct((n_tok, LANES), jnp.int32), jax.ShapeDtypeStruct((1, LANES), F32)],
        scratch_shapes=[pltpu.VMEM((1, LANES), F32), pltpu.VMEM((1, LANES), F32)],
        compiler_params=_params(2),
        name="route_rank",
    )(top_i)
    n_blocks = -(-(n_tok * TOP_K) // rows) + N_EXPERTS
    counts = cnt[0, :N_EXPERTS].astype(jnp.int32)
    padded_end = jnp.cumsum((counts + rows - 1) // rows * rows)
    block_start = jnp.arange(n_blocks, dtype=jnp.int32) * rows
    block_e = jnp.sum((padded_end[None, :] <= block_start[:, None]).astype(jnp.int32), axis=1)
    block_e = jnp.minimum(block_e, N_EXPERTS - 1)
    n_used = (padded_end[-1] // rows).astype(jnp.int32).reshape(1)
    return dest[:, :TOP_K], block_e, n_used, n_blocks * rows


def _dispatch_body(tm, dst_ref, h_ref, xin_ref, x_hbm, sem):
    del xin_ref

    def body(r, carry):
        for kk in range(TOP_K):
            d = dst_ref[0, r * TOP_K + kk]
            pltpu.make_async_copy(h_ref.at[pl.ds(r, 1), :], x_hbm.at[pl.ds(d, 1), :],
                                  sem.at[kk % 2]).start(priority=kk % 2)
        return carry

    lax.fori_loop(0, tm, body, 0, unroll=8)
    for kk in range(TOP_K):
        pltpu.make_async_copy(h_ref, h_ref, sem.at[kk % 2]).wait()


def _dispatch(grp, h2, dest, x_buf):
    tm = grp.tm
    dblk = dest.reshape(grp.tiles, 1, tm * TOP_K)
    return pl.pallas_call(
        functools.partial(_dispatch_body, tm),
        grid=(grp.tiles,),
        in_specs=[pl.BlockSpec((None, 1, tm * TOP_K), lambda i: (i, 0, 0), memory_space=pltpu.SMEM),
                  _row_spec(grp, D_MODEL), pl.BlockSpec(memory_space=pl.ANY)],
        out_specs=pl.BlockSpec(memory_space=pl.ANY),
        out_shape=jax.ShapeDtypeStruct(x_buf.shape, F32),
        scratch_shapes=[pltpu.SemaphoreType.DMA((2,))],
        input_output_aliases={2: 0},
        compiler_params=pltpu.CompilerParams(dimension_semantics=("arbitrary",), vmem_limit_bytes=VMEM_LIMIT,
                                             has_side_effects=True),
        name="dispatch",
    )(dblk, h2, x_buf)


def _expert_body(be_ref, nu_ref, x_ref, wgu_ref, bgu_ref, wdn_ref, bdn_ref, o_ref, wgu_b, wdn_b):
    i = pl.program_id(0)

    @pl.when(i < nu_ref[0])
    def _():
        prev_e = be_ref[jnp.maximum(i - 1, 0)]

        @pl.when((i == 0) | (be_ref[i] != prev_e))
        def _():
            wgu_b[...] = wgu_ref[...].astype(BF16)
            wdn_b[...] = wdn_ref[...].astype(BF16)

        x = x_ref[...].astype(BF16)
        gu = jnp.dot(x, wgu_b[...], preferred_element_type=F32) + bgu_ref[...]
        gate = jnp.minimum(gu[:, :D_FF], SWIGLU_LIMIT)
        lin = jnp.clip(gu[:, D_FF:], -SWIGLU_LIMIT, SWIGLU_LIMIT)
        act = gate * jax.nn.sigmoid(SWIGLU_ALPHA * gate) * (lin + 1.0)
        o_ref[...] = jnp.dot(act.astype(BF16), wdn_b[...], preferred_element_type=F32) + bdn_ref[...]

    @pl.when(i >= nu_ref[0])
    def _():
        o_ref[...] = jnp.zeros(o_ref.shape, F32)


def _experts(layer, x_buf, block_e, n_used, w_gu, b_gu, w_dn, b_dn):
    n_rows = x_buf.shape[0]
    rows = EXPERT_ROWS
    n_blocks = n_rows // rows

    def blk(i, be, nu):
        return jnp.minimum(i, nu[0] - 1)

    def wmap(i, be, nu):
        return (layer, be[blk(i, be, nu)], 0, 0)

    grid_spec = pltpu.PrefetchScalarGridSpec(
        num_scalar_prefetch=2,
        grid=(n_blocks,),
        in_specs=[pl.BlockSpec((rows, D_MODEL), lambda i, be, nu: (blk(i, be, nu), 0)),
                  pl.BlockSpec((None, None, D_MODEL, 2 * D_FF), wmap),
                  pl.BlockSpec((None, None, 1, 2 * D_FF), wmap),
                  pl.BlockSpec((None, None, D_FF, D_MODEL), wmap),
                  pl.BlockSpec((None, None, 1, D_MODEL), wmap)],
        out_specs=pl.BlockSpec((rows, D_MODEL), lambda i, be, nu: (i, 0)),
        scratch_shapes=[pltpu.VMEM((D_MODEL, 2 * D_FF), BF16), pltpu.VMEM((D_FF, D_MODEL), BF16)])
    return pl.pallas_call(
        _expert_body,
        grid_spec=grid_spec,
        out_shape=jax.ShapeDtypeStruct((n_rows, D_MODEL), F32),
        compiler_params=_params(1),
        name="experts",
    )(block_e, n_used, x_buf, w_gu, b_gu, w_dn, b_dn)


def _combine_body(tm, dcur_ref, dnxt_ref, x1_ref, gt_ref, g2_ref, lg_ref, lb_ref, y_hbm, o_ref, ybuf, sem):
    i = pl.program_id(0)
    slot = i % 2

    def gather(dref, sl):
        def body(r, carry):
            for kk in range(TOP_K):
                d = dref[0, r * TOP_K + kk]
                pltpu.make_async_copy(y_hbm.at[pl.ds(d, 1), :], ybuf.at[sl, kk, pl.ds(r, 1), :],
                                      sem.at[sl, kk % 2]).start(priority=kk % 2)
            return carry
        lax.fori_loop(0, tm, body, 0, unroll=8)

    @pl.when(i == 0)
    def _():
        gather(dcur_ref, 0)

    @pl.when(i + 1 < pl.num_programs(0))
    def _():
        gather(dnxt_ref, 1 - slot)

    for kk in range(TOP_K):
        pltpu.make_async_copy(ybuf.at[slot, kk], ybuf.at[slot, kk], sem.at[slot, kk % 2]).wait()
    gt = gt_ref[...]
    ff = gt[:, 0:1] * ybuf[slot, 0]
    for kk in range(1, TOP_K):
        ff = ff + gt[:, kk:kk + 1] * ybuf[slot, kk]
    xa = DEEPNORM_ALPHA * x1_ref[...] + g2_ref[...] * ff
    o_ref[...] = _layernorm(xa) * lg_ref[...] + lb_ref[...]


def _combine(grp, x1, gates, dest, y_buf, mod, lg, lb):
    tm = grp.tm
    last = grp.tiles - 1
    dblk = dest.reshape(grp.tiles, 1, tm * TOP_K)
    dspec = lambda f: pl.BlockSpec((None, 1, tm * TOP_K), f, memory_space=pltpu.SMEM)
    return pl.pallas_call(
        functools.partial(_combine_body, tm),
        grid=(grp.tiles,),
        in_specs=[dspec(lambda i: (i, 0, 0)), dspec(lambda i: (jnp.minimum(i + 1, last), 0, 0)),
                  _row_spec(grp, D_MODEL), _row_spec(grp, LANES), _mod_spec(grp, 5),
                  _const_spec(lg.shape), _const_spec(lb.shape), pl.BlockSpec(memory_space=pl.ANY)],
        out_specs=_row_spec(grp, D_MODEL),
        out_shape=jax.ShapeDtypeStruct((grp.rows, D_MODEL), F32),
        scratch_shapes=[pltpu.VMEM((2, TOP_K, tm, D_MODEL), F32), pltpu.SemaphoreType.DMA((2, 2))],
        compiler_params=_params(1),
        name="combine_ln",
    )(dblk, dblk, x1, gates, mod, lg, lb, y_buf)


def kernel(x_prompt, x_sample, state_wkv, state_shift, state_conv, c_prompt, c_sample, w_in, tm_mu, w0, w2, a0, a2, g2, v0, v1, v2, k_k, k_a, r_k, lnx_g, lnx_b, conv_w, conv_b, conv_ln_g, conv_ln_b, w_out, ada_w, ada_b, ln1_g, ln1_b, ln2_g, ln2_b, router_w, router_b, w_gu, b_gu, w_dn, b_dn):
    bp, tp, _ = x_prompt.shape
    bs, ts, _ = x_sample.shape
    depth = w_in.shape[0]
    n_p, n_s = bp * tp, bs * ts
    n_tok = n_p + n_s
    gp = _Group(n_p, tp, 1, False)
    gs = _Group(n_s, n_s, bs, True)

    mod_all = _ada(jnp.concatenate([c_prompt, c_sample], axis=0), ada_w, ada_b)

    w_in_b = w_in.astype(BF16)
    w_out_b = w_out.astype(BF16)
    hid = jnp.arange(RWKV_DIM) // HEAD_DIM
    gmat = (hid[:, None] == hid[None, :]).astype(BF16)
    zero_wa = jnp.zeros((depth, DECAY_LORA, RWKV_DIM), F32)
    wwa = jnp.concatenate([jnp.concatenate([w2, zero_wa], axis=2),
                           jnp.concatenate([zero_wa, a2], axis=2)], axis=1).astype(BF16)
    v1p = jnp.pad(v1, ((0, 0), (0, 0), (0, LANES - VRES_LORA))).astype(BF16)
    v2p = jnp.pad(v2, ((0, 0), (0, LANES - VRES_LORA), (0, 0))).astype(BF16)
    rw_p = jnp.pad(router_w, ((0, 0), (0, 0), (0, LANES - N_EXPERTS)))
    rw_hi = rw_p.astype(BF16)
    rw_lo = (rw_p - rw_hi.astype(F32)).astype(BF16)
    rb_p = jnp.pad(router_b, ((0, 0), (0, LANES - N_EXPERTS)), constant_values=-1e30)
    b_gu4 = b_gu.reshape(depth, N_EXPERTS, 1, 2 * D_FF)
    b_dn4 = b_dn.reshape(depth, N_EXPERTS, 1, D_MODEL)

    def row2(a):
        return a.reshape(1, -1)

    x_p = x_prompt.reshape(n_p, D_MODEL)
    x_s = jnp.transpose(x_sample, (1, 0, 2)).reshape(n_s, D_MODEL)
    zeros_wkv = jnp.zeros((bp, RWKV_HEADS, HEAD_DIM, HEAD_DIM), F32)
    zeros_shift = jnp.zeros((bp, 1, SHIFT_DIM), F32)
    zeros_conv = jnp.zeros((bp, CONV_HIST, CONV_DIM), F32)
    wkv_s_t = jnp.transpose(state_wkv, (0, 2, 4, 3, 1))

    vf_p = vf_s = None
    wkv_p_out, wkv_s_out, shift_p_out, shift_s_out, conv_p_out, conv_s_out = [], [], [], [], [], []
    for l in range(depth):
        mod_p = mod_all[l, :bp].reshape(bp, 1, 6 * D_MODEL)
        mod_s = jnp.tile(mod_all[l, bp:], (ts, 1))
        wts = {"mu": row2(tm_mu[l]), "w0": row2(w0[l]), "a0": row2(a0[l]), "k_k": row2(k_k[l]),
               "k_a": row2(k_a[l]), "wwa": wwa[l], "g2": g2[l].astype(BF16),
               "lnx_g": row2(lnx_g[l]), "lnx_b": row2(lnx_b[l]), "r_k": row2(r_k[l]), "w_out": w_out_b[l],
               "ln1_g": row2(ln1_g[l]), "ln1_b": row2(ln1_b[l]), "router_w": rw_hi[l], "router_w_lo": rw_lo[l],
               "router_b": row2(rb_p[l])}
        if l > 0:
            wts.update({"v0": row2(v0[l - 1]), "v1": v1p[l - 1], "v2": v2p[l - 1]})

        per_group = []
        for grp, x, mod, is_p in ((gp, x_p, mod_p, True), (gs, x_s, mod_s, False)):
            zs, zc = _inproj(grp, x, mod, w_in_b[l])
            if is_p:
                shift0, conv0 = zeros_shift, zeros_conv
            else:
                shift0 = state_shift[l][None]
                conv0 = jnp.transpose(state_conv[l], (1, 0, 2)).reshape(1, CONV_HIST * bs, CONV_DIM)
            vfirst = vf_p if is_p else vf_s
            r, lw, k, v, an, bn, g = _prep(grp, zs, shift0, vfirst if l > 0 else None, wts, gmat)
            if l == 0:
                if is_p:
                    vf_p = v
                else:
                    vf_s = v
            if is_p:
                y, s_new = _wkv_chunked(bp, tp, r, lw, k, v, an, bn, zeros_wkv)
                wkv_p_out.append(s_new)
                shift_p_out.append(zs.reshape(bp, tp, SHIFT_DIM)[:, -1])
            else:
                tmaj = lambda a: jnp.transpose(a.reshape(ts, bs, RWKV_DIM), (0, 2, 1))
                y_t, s_new = _wkv_lanes(ts, tmaj(r), tmaj(lw), tmaj(k), tmaj(v), tmaj(an), tmaj(bn), wkv_s_t[l])
                y = jnp.transpose(y_t, (0, 2, 1)).reshape(n_s, RWKV_DIM)
                wkv_s_out.append(jnp.transpose(s_new, (3, 0, 2, 1)))
                shift_s_out.append(zs[(ts - 1) * bs:])
            cv, tail = _conv(grp, zc, conv0, conv_w[l], row2(conv_b[l]), row2(conv_ln_g[l]), row2(conv_ln_b[l]))
            if is_p:
                conv_p_out.append(tail)
            else:
                conv_s_out.append(jnp.transpose(tail.reshape(CONV_HIST, bs, CONV_DIM), (1, 0, 2)))
            off = 0 if is_p else n_p
            x1, h2, ti, gt = _mix(grp, x, y, r, k, v, g, cv, mod, wts, gmat)
            per_group.append((grp, x1, h2, ti, gt, mod, off))

        top_i = jnp.concatenate([pg[3] for pg in per_group], axis=0)
        dest, block_e, n_used, n_rows = _route(top_i, gp.tm)
        if l == 0:
            x_buf = jnp.zeros((n_rows, D_MODEL), F32)
        for grp, x1, h2, ti, gt, mod, off in per_group:
            x_buf = _dispatch(grp, h2, dest[off:off + grp.rows], x_buf)
        y_buf = _experts(l, x_buf, block_e, n_used, w_gu, b_gu4, w_dn, b_dn4)
        outs = []
        for grp, x1, h2, ti, gt, mod, off in per_group:
            outs.append(_combine(grp, x1, gt, dest[off:off + grp.rows], y_buf, mod, row2(ln2_g[l]), row2(ln2_b[l])))
        x_p, x_s = outs

    y_prompt = x_p.reshape(bp, tp, D_MODEL)
    y_sample = jnp.transpose(x_s.reshape(ts, bs, D_MODEL), (1, 0, 2))
    return (y_prompt, y_sample, jnp.stack(wkv_p_out), jnp.stack(wkv_s_out), jnp.stack(shift_p_out),
            jnp.stack(shift_s_out), jnp.stack(conv_p_out), jnp.stack(conv_s_out))
```

```python
import functools

import jax
import jax.numpy as jnp
from jax import lax
from jax.experimental import pallas as pl
from jax.experimental.pallas import tpu as pltpu

F32 = jnp.float32
BF16 = jnp.bfloat16

D_MODEL = 1024
DEPTH = 4
HEAD_DIM = 64
RWKV_DIM = 512
RWKV_HEADS = RWKV_DIM // HEAD_DIM
CONV_DIM = 512
CONV_WIDTH = 31
CONV_HIST = CONV_WIDTH - 1
DECAY_LORA = 64
AAA_LORA = 64
GATE_LORA = 128
VRES_LORA = 32
SHIFT_DIM = 3 * RWKV_DIM + DECAY_LORA + AAA_LORA + GATE_LORA
PROJ_DIM = SHIFT_DIM + 2 * CONV_DIM
N_EXPERTS = 32
TOP_K = 4
D_FF = D_MODEL
SWIGLU_LIMIT = 7.0
SWIGLU_ALPHA = 1.702
LN_EPS = 1e-5
GN_EPS = 64e-5
DEEPNORM_ALPHA = (2 * DEPTH) ** 0.25

LANES = 128
SUBLANES = 8
ROW_TILE = 512
CHUNK = 64
WKV_BATCHES_PER_STEP = 2
EXPERT_ROWS = 256
VMEM_LIMIT = 56 * 1024 * 1024


def _params(n_axes=1, vmem=VMEM_LIMIT):
    return pltpu.CompilerParams(dimension_semantics=("arbitrary",) * n_axes, vmem_limit_bytes=vmem)


def _round_up(x, m):
    return (x + m - 1) // m * m


def _bdot(a, b):
    return jnp.dot(a.astype(BF16), b.astype(BF16), preferred_element_type=F32)


def _split3(x):
    hi = x.astype(BF16)
    r1 = x - hi.astype(F32)
    mid = r1.astype(BF16)
    lo = (r1 - mid.astype(F32)).astype(BF16)
    return hi, mid, lo


def _headsum(x, g):
    return jnp.dot(x.astype(BF16), g, preferred_element_type=F32)


LANE_TILES = D_MODEL // LANES
assert LANE_TILES == SUBLANES


def _store_row_tiles(ref, x, rows, lead=()):
    for c in range(LANE_TILES):
        ref[lead + (pl.ds(c, rows, stride=LANE_TILES), slice(None))] = x[:, c * LANES:(c + 1) * LANES]


def _load_row_tiles(ref, rows, lead=()):
    return [ref[lead + (pl.ds(c, rows, stride=LANE_TILES), slice(None))] for c in range(LANE_TILES)]


def _row_tile(ref, r, lead=()):
    return ref.at[lead + (pl.ds(pl.multiple_of(r * LANE_TILES, LANE_TILES), LANE_TILES), slice(None))]


def _layernorm(x):
    mu = jnp.mean(x, axis=-1, keepdims=True)
    xc = x - mu
    var = jnp.mean(xc * xc, axis=-1, keepdims=True)
    return xc * lax.rsqrt(var + LN_EPS)


def _ada_body(c_ref, w_ref, b_ref, o_ref):
    c = c_ref[...]
    cs = c * jax.nn.sigmoid(c)
    o_ref[...] = _bdot(cs, w_ref[...]) + b_ref[...]


def _ada(c_all, ada_w, ada_b):
    depth, _, width = ada_w.shape
    nb = c_all.shape[0]
    tn = 1536
    return pl.pallas_call(
        _ada_body,
        grid=(depth, width // tn),
        in_specs=[pl.BlockSpec((nb, D_MODEL), lambda l, j: (0, 0)),
                  pl.BlockSpec((None, D_MODEL, tn), lambda l, j: (l, 0, j)),
                  pl.BlockSpec((None, 1, tn), lambda l, j: (l, 0, j))],
        out_specs=pl.BlockSpec((None, nb, tn), lambda l, j: (l, 0, j)),
        out_shape=jax.ShapeDtypeStruct((depth, nb, width), F32),
        compiler_params=_params(2),
        name="ada_mod",
    )(c_all, ada_w, ada_b.reshape(depth, 1, width))


class _Group:
    def __init__(self, rows, seg_len, stride, mod_per_row):
        self.rows = rows
        self.seg_len = seg_len
        self.stride = stride
        self.mod_per_row = mod_per_row
        self.tm = min(ROW_TILE, rows)
        assert rows % self.tm == 0 and seg_len % self.tm == 0
        self.tiles = rows // self.tm
        self.tiles_per_seg = seg_len // self.tm
        self.nseg = rows // seg_len


def _mod_spec(grp, which):
    if grp.mod_per_row:
        return pl.BlockSpec((grp.tm, D_MODEL), lambda i: (i, which))
    tps = grp.tiles_per_seg
    return pl.BlockSpec((None, 1, D_MODEL), lambda i: (i // tps, 0, which))


def _row_spec(grp, width):
    return pl.BlockSpec((grp.tm, width), lambda i: (i, 0))


def _const_spec(shape):
    nd = len(shape)
    return pl.BlockSpec(shape, lambda i: (0,) * nd)


def _inproj_body(x_ref, sc_ref, sh_ref, w_ref, zs_ref, zc_ref):
    h = x_ref[...] * (1.0 + sc_ref[...]) + sh_ref[...]
    z = jnp.dot(h.astype(BF16), w_ref[...], preferred_element_type=F32)
    zs_ref[...] = z[:, :SHIFT_DIM]
    zc_ref[...] = z[:, SHIFT_DIM:]


def _inproj(grp, x, mod, w_in_b):
    return pl.pallas_call(
        _inproj_body,
        grid=(grp.tiles,),
        in_specs=[_row_spec(grp, D_MODEL), _mod_spec(grp, 1), _mod_spec(grp, 0),
                  _const_spec((D_MODEL, PROJ_DIM))],
        out_specs=[_row_spec(grp, SHIFT_DIM), _row_spec(grp, 2 * CONV_DIM)],
        out_shape=[jax.ShapeDtypeStruct((grp.rows, SHIFT_DIM), F32),
                   jax.ShapeDtypeStruct((grp.rows, 2 * CONV_DIM), F32)],
        compiler_params=_params(1),
        name="inproj",
    )(x, mod, mod, w_in_b)


def _prep_body(has_vres, grp, halo, *refs):
    it = iter(refs)
    zs_ref, sh0_ref = next(it), next(it)
    vf_ref = next(it) if has_vres else None
    mu_ref, w0_ref, a0_ref, kk_ref, ka_ref, wwa_ref, g2_ref = (next(it) for _ in range(7))
    if has_vres:
        v0_ref, v1_ref, v2_ref = next(it), next(it), next(it)
    gmat_ref = next(it)
    r_o, lw_o, k_o, v_o, an_o, bn_o, g_o = (next(it) for _ in range(7))
    ext = next(it)

    tm, s = grp.tm, grp.stride
    j = pl.program_id(0) % grp.tiles_per_seg

    @pl.when(j == 0)
    def _():
        ext[pl.ds(halo - s, s), :] = sh0_ref[...]

    zs = zs_ref[...]
    ext[pl.ds(halo, tm), :] = zs
    prev = ext[pl.ds(halo - s, tm), :]
    if grp.tiles_per_seg > 1:
        ext[pl.ds(0, halo), :] = ext[pl.ds(tm, halo), :]

    zm = zs + (prev - zs) * mu_ref[...]
    i1, i2, i3 = RWKV_DIM, 2 * RWKV_DIM, 3 * RWKV_DIM
    i5 = i3 + DECAY_LORA + AAA_LORA
    r, k, v = zm[:, :i1], zm[:, i1:i2], zm[:, i2:i3]
    zwa, zg = zm[:, i3:i5], zm[:, i5:]
    lane = lax.broadcasted_iota(jnp.int32, zwa.shape, 1)
    xwa = jnp.where(lane < DECAY_LORA, jnp.tanh(zwa), zwa)
    wa = _bdot(xwa, wwa_ref[...])
    w_in = w0_ref[...] + wa[:, :RWKV_DIM]
    neg = -w_in
    softplus = jnp.maximum(neg, 0.0) + jnp.log(1.0 + jnp.exp(-jnp.abs(neg)))
    w_log = -softplus - 0.5
    lw = -jnp.exp(w_log)
    a = jax.nn.sigmoid(a0_ref[...] + wa[:, RWKV_DIM:])
    g = _bdot(jax.nn.sigmoid(zg), g2_ref[...])
    if has_vres:
        vl = _bdot(_bdot(v, v1_ref[...]), v2_ref[...])
        vg = jax.nn.sigmoid(v0_ref[...] + vl)
        v = v + (vf_ref[...] - v) * vg
    kk = k * kk_ref[...]
    ss = _headsum(kk * kk, gmat_ref[...])
    kk = kk * lax.rsqrt(jnp.maximum(ss, 1e-24))
    k = k * (1.0 + (a - 1.0) * ka_ref[...])
    r_o[...] = r
    lw_o[...] = lw
    k_o[...] = k
    v_o[...] = v
    an_o[...] = -kk
    bn_o[...] = kk * a
    g_o[...] = g


def _prep(grp, zs, shift0, vfirst, wts, gmat):
    has_vres = vfirst is not None
    halo = _round_up(grp.stride, SUBLANES)
    s = grp.stride
    tps = grp.tiles_per_seg
    in_specs = [_row_spec(grp, SHIFT_DIM),
                pl.BlockSpec((None, s, SHIFT_DIM), lambda i: (i // tps, 0, 0))]
    args = [zs, shift0]
    if has_vres:
        in_specs.append(_row_spec(grp, RWKV_DIM))
        args.append(vfirst)
    names = ["mu", "w0", "a0", "k_k", "k_a", "wwa", "g2"] + (["v0", "v1", "v2"] if has_vres else [])
    for nm in names:
        in_specs.append(_const_spec(wts[nm].shape))
        args.append(wts[nm])
    in_specs.append(_const_spec(gmat.shape))
    args.append(gmat)
    out = jax.ShapeDtypeStruct((grp.rows, RWKV_DIM), F32)
    return pl.pallas_call(
        functools.partial(_prep_body, has_vres, grp, halo),
        grid=(grp.tiles,),
        in_specs=in_specs,
        out_specs=[_row_spec(grp, RWKV_DIM)] * 7,
        out_shape=[out] * 7,
        scratch_shapes=[pltpu.VMEM((halo + grp.tm, SHIFT_DIM), F32)],
        compiler_params=_params(1),
        name="rwkv_prep",
    )(*args)


def _nt(a, b):
    return lax.dot_general(a.astype(BF16), b.astype(BF16), (((1,), (1,)), ((), ())),
                           preferred_element_type=F32)


def _tn(a, b):
    return lax.dot_general(a.astype(BF16), b.astype(BF16), (((0,), (0,)), ((), ())),
                           preferred_element_type=F32)


def _wkv_chunk_body(nchunks, nb, r_ref, lw_ref, k_ref, v_ref, a_ref, b_ref, s0_ref, y_ref, so_ref, st):
    c = pl.program_id(1)
    L, N = CHUNK, HEAD_DIM

    @pl.when(c == 0)
    def _():
        st[...] = s0_ref[...]

    P2, W2 = 2 * L, 2 * N
    assert L == N and W2 == LANES
    trow = lax.broadcasted_iota(jnp.int32, (L, L), 0)
    tcol = lax.broadcasted_iota(jnp.int32, (L, L), 1)
    tri = jnp.where(trow >= tcol, 1.0, 0.0).astype(BF16)
    row = lax.broadcasted_iota(jnp.int32, (P2, P2), 0)
    col = lax.broadcasted_iota(jnp.int32, (P2, P2), 1)
    lg = L.bit_length() - 1
    same_head = (row >> lg) == (col >> lg)
    strict = same_head & (row > col)
    incl = same_head & (row >= col)
    eye = row == col
    sh = 3
    diag_blk = (row >> sh) == (col >> sh)
    off_masks = []
    while (1 << sh) < L:
        off_masks.append(((row >> sh) == (col >> sh) + 1) & ((row >> (sh + 1)) == (col >> (sh + 1))))
        sh += 1

    lane_lo = lax.broadcasted_iota(jnp.int32, (L, W2), 1) < N
    npair = RWKV_HEADS // 2

    def stack(x, p):
        xp = x[:, p * W2:(p + 1) * W2]
        return jnp.concatenate([jnp.where(lane_lo, xp, 0.0), jnp.where(lane_lo, 0.0, xp)], axis=0)

    at, rt, bt, kt, bh, kh, vv, p_last = [], [], [], [], [], [], [], []
    for bi in range(nb):
        lw = lw_ref[bi]
        h3 = _split3(lw)
        cl = (jnp.dot(tri, h3[0], preferred_element_type=F32) + jnp.dot(tri, h3[1], preferred_element_type=F32)
              + jnp.dot(tri, h3[2], preferred_element_type=F32))
        cl_last = cl[L - 1:L, :]
        ecl = jnp.exp(cl)
        encl = jnp.exp(-cl)
        dl = jnp.exp(cl_last - cl)
        pl_row = jnp.exp(cl_last)
        an, bn, kx, rx, vx = a_ref[bi], b_ref[bi], k_ref[bi], r_ref[bi], v_ref[bi]
        at_all, rt_all = an * jnp.exp(cl - lw), rx * ecl
        bt_all, kt_all, bh_all, kh_all = bn * encl, kx * encl, bn * dl, kx * dl
        for p in range(npair):
            at.append(stack(at_all, p).astype(BF16))
            rt.append(stack(rt_all, p))
            bt.append(stack(bt_all, p).astype(BF16))
            kt.append(stack(kt_all, p).astype(BF16))
            bh.append(stack(bh_all, p).astype(BF16))
            kh.append(stack(kh_all, p).astype(BF16))
            vv.append(stack(vx, p).astype(BF16))
            p_last.append(pl_row[:, p * W2:(p + 1) * W2])
    pairs = range(nb * npair)

    ar = [jnp.concatenate([at[p], rt[p].astype(BF16)], axis=0) for p in pairs]
    xb = [_nt(ar[p], bt[p]) for p in pairs]
    xk = [_nt(ar[p], kt[p]) for p in pairs]
    a_ab = [jnp.where(strict, xb[p][:P2], 0.0) for p in pairs]
    a_ak = [jnp.where(strict, xk[p][:P2], 0.0) for p in pairs]
    a_rb = [jnp.where(incl, xb[p][P2:], 0.0).astype(BF16) for p in pairs]
    a_rk = [jnp.where(incl, xk[p][P2:], 0.0) for p in pairs]
    pw = [jnp.where(diag_blk, a_ab[p], 0.0) for p in pairs]
    tm = [jnp.where(eye, 1.0, 0.0) + pw[p] for p in pairs]
    for _ in range(2):
        pw = [_bdot(pw[p], pw[p]) for p in pairs]
        tm = [tm[p] + _bdot(tm[p], pw[p]) for p in pairs]
    for off in off_masks:
        t1 = [_bdot(jnp.where(off, a_ab[p], 0.0), tm[p]) for p in pairs]
        tm = [tm[p] + _bdot(tm[p], t1[p]) for p in pairs]
    akv = [_bdot(a_ak[p], vv[p]).astype(BF16) for p in pairs]
    wu = [_bdot(tm[p], jnp.concatenate([at[p], akv[p]], axis=1)) for p in pairs]
    qy = [jnp.dot(a_rb[p], wu[p].astype(BF16), preferred_element_type=F32) for p in pairs]
    rkv = [_bdot(a_rk[p], vv[p]) for p in pairs]
    ys = []
    for p in pairs:
        bi, pp = divmod(p, npair)
        s_p = st[bi, pp]
        w, u0 = wu[p][:, :W2], wu[p][:, W2:]
        q = rt[p] + qy[p][:, :W2]
        ybd = _nt(q, s_p) + qy[p][:, W2:] + rkv[p]
        ys.append(ybd[:L] + ybd[L:])
        cm = _tn(u0, bh[p]) + _tn(vv[p], kh[p])
        st[bi, pp] = s_p * p_last[p] + _bdot(s_p, _tn(w, bh[p])) + cm
    for bi in range(nb):
        y_ref[bi] = jnp.concatenate(ys[bi * npair:(bi + 1) * npair], axis=1)

    @pl.when(c == nchunks - 1)
    def _():
        so_ref[...] = st[...]


def _wkv_chunked(nbatch, tlen, r, lw, k, v, an, bn, s0):
    nchunks = tlen // CHUNK
    npair, w2 = RWKV_HEADS // 2, 2 * HEAD_DIM
    s5 = s0.reshape(nbatch, npair, 2, HEAD_DIM, HEAD_DIM)
    zero = jnp.zeros_like(s5[:, :, 0])
    s_bd = jnp.concatenate([jnp.concatenate([s5[:, :, 0], zero], axis=-1),
                            jnp.concatenate([zero, s5[:, :, 1]], axis=-1)], axis=-2)
    nb = WKV_BATCHES_PER_STEP if nbatch % WKV_BATCHES_PER_STEP == 0 else 1
    tok = pl.BlockSpec((nb, CHUNK, RWKV_DIM), lambda b, c: (b, c, 0))
    stt = pl.BlockSpec((nb, npair, w2, w2), lambda b, c: (b, 0, 0, 0))
    seq = lambda t: t.reshape(nbatch, tlen, RWKV_DIM)
    y, so = pl.pallas_call(
        functools.partial(_wkv_chunk_body, nchunks, nb),
        grid=(nbatch // nb, nchunks),
        in_specs=[tok] * 6 + [stt],
        out_specs=[tok, stt],
        out_shape=[jax.ShapeDtypeStruct((nbatch, tlen, RWKV_DIM), F32),
                   jax.ShapeDtypeStruct((nbatch, npair, w2, w2), F32)],
        scratch_shapes=[pltpu.VMEM((nb, npair, w2, w2), F32)],
        compiler_params=_params(2),
        name="wkv_chunked",
    )(seq(r), seq(lw), seq(k), seq(v), seq(an), seq(bn), s_bd)
    s_new = jnp.stack([so[:, :, :HEAD_DIM, :HEAD_DIM], so[:, :, HEAD_DIM:, HEAD_DIM:]], axis=2)
    return y.reshape(nbatch * tlen, RWKV_DIM), s_new.reshape(nbatch, RWKV_HEADS, HEAD_DIM, HEAD_DIM)


def _wkv_lane_body(tlen, r_ref, lw_ref, k_ref, v_ref, a_ref, b_ref, s_ref, y_ref, so_ref):
    N = HEAD_DIM
    nb = s_ref.shape[-1]

    def row(ref, t, j):
        return jnp.broadcast_to(ref[t, pl.ds(j, 1), :], (N, nb))

    def first(j, acc):
        return acc + s_ref[j] * row(a_ref, 0, j)

    sa = lax.fori_loop(0, N, first, jnp.zeros((N, nb), F32))
    for t in range(tlen):
        v_t = v_ref[t]
        src = s_ref if t == 0 else so_ref

        def body(j, carry, t=t, v_t=v_t, src=src, sa=sa):
            y, sa_next = carry
            sj = src[j] * jnp.exp(row(lw_ref, t, j)) + sa * row(b_ref, t, j) + v_t * row(k_ref, t, j)
            so_ref[j] = sj
            y = y + sj * row(r_ref, t, j)
            if t + 1 < tlen:
                sa_next = sa_next + sj * row(a_ref, t + 1, j)
            return y, sa_next

        zero = jnp.zeros((N, nb), F32)
        y, sa = lax.fori_loop(0, N, body, (zero, zero))
        y_ref[t] = y


def _wkv_lanes(tlen, r, lw, k, v, an, bn, s_t):
    nb = r.shape[-1]
    tok = pl.BlockSpec((tlen, HEAD_DIM, nb), lambda h: (0, h, 0))
    stt = pl.BlockSpec((None, HEAD_DIM, HEAD_DIM, nb), lambda h: (h, 0, 0, 0))
    return pl.pallas_call(
        functools.partial(_wkv_lane_body, tlen),
        grid=(RWKV_HEADS,),
        in_specs=[tok] * 6 + [stt],
        out_specs=[tok, stt],
        out_shape=[jax.ShapeDtypeStruct((tlen, RWKV_DIM, nb), F32),
                   jax.ShapeDtypeStruct((RWKV_HEADS, HEAD_DIM, HEAD_DIM, nb), F32)],
        compiler_params=_params(1),
        name="wkv_lanes",
    )(r, lw, k, v, an, bn, s_t)


def _conv_body(grp, halo, zc_ref, c0_ref, cw_ref, cb_ref, lg_ref, lb_ref, cv_ref, tail_ref, ext, win):
    tm, s = grp.tm, grp.stride
    hist = CONV_HIST * s
    j = pl.program_id(0) % grp.tiles_per_seg

    @pl.when(j == 0)
    def _():
        ext[pl.ds(halo - hist, hist), :] = c0_ref[...]

    zc = zc_ref[...]
    u = zc[:, :CONV_DIM] * jax.nn.sigmoid(zc[:, CONV_DIM:])
    ext[pl.ds(halo, tm), :] = u
    acc = jnp.broadcast_to(cb_ref[...], (tm, CONV_DIM))
    offs = [halo - hist + w * s for w in range(CONV_WIDTH)]
    for phase in range(SUBLANES):
        taps = [w for w in range(CONV_WIDTH) if offs[w] % SUBLANES == phase]
        if not taps:
            continue
        if s % SUBLANES == 0:
            for w in taps:
                acc = acc + cw_ref[pl.ds(w, 1), :] * ext[pl.ds(offs[w], tm), :]
            continue
        lo = offs[taps[0]]
        span = tm + offs[taps[-1]] - lo
        win[pl.ds(0, span), :] = ext[pl.ds(lo, span), :]
        for w in taps:
            acc = acc + cw_ref[pl.ds(w, 1), :] * win[pl.ds(offs[w] - lo, tm), :]
    y = _layernorm(acc) * lg_ref[...] + lb_ref[...]
    cv_ref[...] = y * jax.nn.sigmoid(y)
    tail_ref[...] = ext[pl.ds(halo + tm - hist, hist), :]
    if grp.tiles_per_seg > 1:
        ext[pl.ds(0, halo), :] = ext[pl.ds(tm, halo), :]


def _conv(grp, zc, conv0, cw, cb, lg, lb):
    hist = CONV_HIST * grp.stride
    halo = _round_up(hist, SUBLANES)
    tps = grp.tiles_per_seg
    hist_spec = pl.BlockSpec((None, hist, CONV_DIM), lambda i: (i // tps, 0, 0))
    return pl.pallas_call(
        functools.partial(_conv_body, grp, halo),
        grid=(grp.tiles,),
        in_specs=[_row_spec(grp, 2 * CONV_DIM), hist_spec,
                  _const_spec(cw.shape), _const_spec(cb.shape), _const_spec(lg.shape), _const_spec(lb.shape)],
        out_specs=[_row_spec(grp, CONV_DIM), hist_spec],
        out_shape=[jax.ShapeDtypeStruct((grp.rows, CONV_DIM), F32),
                   jax.ShapeDtypeStruct((grp.nseg, hist, CONV_DIM), F32)],
        scratch_shapes=[pltpu.VMEM((halo + grp.tm, CONV_DIM), F32),
                        pltpu.VMEM((grp.tm + _round_up(CONV_HIST, SUBLANES), CONV_DIM), F32)],
        compiler_params=_params(1),
        name="conv_group",
    )(zc, conv0, cw, cb, lg, lb)


def _mix_body(x_ref, y_ref, r_ref, k_ref, v_ref, g_ref, cv_ref, g1_ref, sc2_ref, sh2_ref,
              lnxg_ref, lnxb_ref, rk_ref, wout_ref, l1g_ref, l1b_ref, rw_ref, rwl_ref, rb_ref, gmat_ref,
              x1_ref, h2_ref, ti_ref, gt_ref):
    gm = gmat_ref[...]
    y = y_ref[...]
    inv = 1.0 / HEAD_DIM
    ym = _headsum(y, gm) * inv
    yc = y - ym
    yv = _headsum(yc * yc, gm) * inv
    yn = yc * lax.rsqrt(yv + GN_EPS) * lnxg_ref[...] + lnxb_ref[...]
    bonus = _headsum(r_ref[...] * k_ref[...] * rk_ref[...], gm) * v_ref[...]
    yo = (yn + bonus) * g_ref[...]
    mix = (jnp.dot(yo.astype(BF16), wout_ref[pl.ds(0, RWKV_DIM), :], preferred_element_type=F32)
           + jnp.dot(cv_ref[...].astype(BF16), wout_ref[pl.ds(RWKV_DIM, CONV_DIM), :],
                     preferred_element_type=F32))
    xa = DEEPNORM_ALPHA * x_ref[...] + g1_ref[...] * mix
    x1 = _layernorm(xa) * l1g_ref[...] + l1b_ref[...]
    x1_ref[...] = x1
    h2 = x1 * (1.0 + sc2_ref[...]) + sh2_ref[...]
    _store_row_tiles(h2_ref, h2, h2.shape[0])
    h_hi = h2.astype(BF16)
    h_lo = (h2 - h_hi.astype(F32)).astype(BF16)
    logits = (jnp.dot(h_hi, rw_ref[...], preferred_element_type=F32)
              + jnp.dot(h_hi, rwl_ref[...], preferred_element_type=F32)
              + jnp.dot(h_lo, rw_ref[...], preferred_element_type=F32)) + rb_ref[...]
    lane = lax.broadcasted_iota(jnp.int32, logits.shape, 1).astype(F32)
    cur = logits
    vals, idxs = [], []
    for _ in range(TOP_K):
        m = jnp.max(cur, axis=-1, keepdims=True)
        idx = jnp.min(jnp.where(cur == m, lane, float(LANES)), axis=-1, keepdims=True)
        vals.append(m)
        idxs.append(idx)
        cur = jnp.where(lane == idx, -jnp.inf, cur)
    es = [jnp.exp(vv - vals[0]) for vv in vals]
    den = es[0] + es[1] + es[2] + es[3]
    ti = jnp.zeros_like(logits)
    gt = jnp.zeros_like(logits)
    for kk in range(TOP_K):
        ti = jnp.where(lane == float(kk), idxs[kk], ti)
        gt = jnp.where(lane == float(kk), es[kk] / den, gt)
    ti_ref[...] = ti.astype(jnp.int32)
    gt_ref[...] = gt


def _mix(grp, x, y, r, k, v, g, cv, mod, wts, gmat):
    half = _row_spec(grp, RWKV_DIM)
    names = ["lnx_g", "lnx_b", "r_k", "w_out", "ln1_g", "ln1_b", "router_w", "router_w_lo", "router_b"]
    in_specs = ([_row_spec(grp, D_MODEL)] + [half] * 6 + [_mod_spec(grp, 2), _mod_spec(grp, 4), _mod_spec(grp, 3)]
                + [_const_spec(wts[nm].shape) for nm in names] + [_const_spec(gmat.shape)])
    args = [x, y, r, k, v, g, cv, mod, mod, mod] + [wts[nm] for nm in names] + [gmat]
    return pl.pallas_call(
        _mix_body,
        grid=(grp.tiles,),
        in_specs=in_specs,
        out_specs=[_row_spec(grp, D_MODEL), pl.BlockSpec((grp.tm * LANE_TILES, LANES), lambda i: (i, 0)),
                   _row_spec(grp, LANES), _row_spec(grp, LANES)],
        out_shape=[jax.ShapeDtypeStruct((grp.rows, D_MODEL), F32),
                   jax.ShapeDtypeStruct((grp.rows * LANE_TILES, LANES), F32),
                   jax.ShapeDtypeStruct((grp.rows, LANES), jnp.int32),
                   jax.ShapeDtypeStruct((grp.rows, LANES), F32)],
        compiler_params=_params(1),
        name="mix_out",
    )(*args)


def _route_body(ti_ref, dest_ref, cnt_ref, carry, pstart):
    phase = pl.program_id(0)
    i = pl.program_id(1)
    ti = ti_ref[...]
    tm = ti.shape[0]
    lane = lax.broadcasted_iota(jnp.int32, (tm, LANES), 1)
    sel = [lane == ti[:, kk:kk + 1] for kk in range(TOP_K)]
    onehot = jnp.zeros((tm, LANES), F32)
    for kk in range(TOP_K):
        onehot = onehot + jnp.where(sel[kk], 1.0, 0.0)

    @pl.when((phase == 0) & (i == 0))
    def _():
        carry[...] = jnp.zeros(carry.shape, F32)

    @pl.when((phase == 1) & (i == 0))
    def _():
        cnt = carry[...]
        cnt_ref[...] = cnt
        padded = jnp.floor((cnt + (EXPERT_ROWS - 1)) * (1.0 / EXPERT_ROWS)) * EXPERT_ROWS
        er = lax.broadcasted_iota(jnp.int32, (LANES, LANES), 0)
        ec = lax.broadcasted_iota(jnp.int32, (LANES, LANES), 1)
        earlier = jnp.where(er < ec, 1.0, 0.0).astype(BF16)
        h3 = _split3(jnp.broadcast_to(padded, (SUBLANES, LANES)))
        acc = (jnp.dot(h3[0], earlier, preferred_element_type=F32)
               + jnp.dot(h3[1], earlier, preferred_element_type=F32)
               + jnp.dot(h3[2], earlier, preferred_element_type=F32))
        pstart[...] = acc[0:1, :]
        carry[...] = jnp.zeros(carry.shape, F32)

    @pl.when(phase == 1)
    def _():
        rr = lax.broadcasted_iota(jnp.int32, (tm, tm), 0)
        cc = lax.broadcasted_iota(jnp.int32, (tm, tm), 1)
        before = jnp.where(rr > cc, 1.0, 0.0).astype(BF16)
        cum = jnp.dot(before, onehot.astype(BF16), preferred_element_type=F32) + (carry[...] + pstart[...])
        dest = jnp.zeros((tm, LANES), F32)
        for kk in range(TOP_K):
            pk = jnp.sum(jnp.where(sel[kk], cum, 0.0), axis=-1, keepdims=True)
            dest = jnp.where(lane == kk, pk, dest)
        dest_ref[...] = dest.astype(jnp.int32)

    carry[...] = carry[...] + jnp.sum(onehot, axis=0, keepdims=True)


def _route(top_i, tm):
    n_tok = top_i.shape[0]
    rows = EXPERT_ROWS
    dest, cnt = pl.pallas_call(
        _route_body,
        grid=(2, n_tok // tm),
        in_specs=[pl.BlockSpec((tm, LANES), lambda ph, i: (i, 0))],
        out_specs=[pl.BlockSpec((tm, LANES), lambda ph, i: (i * ph, 0)),
                   pl.BlockSpec((1, LANES), lambda ph, i: (0, 0))],
        out_shape=[jax.ShapeDtypeStruct((n_tok, LANES), jnp.int32), jax.ShapeDtypeStruct((1, LANES), F32)],
        scratch_shapes=[pltpu.VMEM((1, LANES), F32), pltpu.VMEM((1, LANES), F32)],
        compiler_params=_params(2),
        name="route_rank",
    )(top_i)
    n_blocks = -(-(n_tok * TOP_K) // rows) + N_EXPERTS
    counts = cnt[0, :N_EXPERTS].astype(jnp.int32)
    padded_end = jnp.cumsum((counts + rows - 1) // rows * rows)
    block_start = jnp.arange(n_blocks, dtype=jnp.int32) * rows
    block_e = jnp.sum((padded_end[None, :] <= block_start[:, None]).astype(jnp.int32), axis=1)
    block_e = jnp.minimum(block_e, N_EXPERTS - 1)
    n_used = (padded_end[-1] // rows).astype(jnp.int32).reshape(1)
    return dest[:, :TOP_K], block_e, n_used, n_blocks * rows


def _dispatch_body(tm, dst_ref, h_ref, xin_ref, x_hbm, sem):
    del xin_ref

    def body(r, carry):
        for kk in range(TOP_K):
            d = dst_ref[0, r * TOP_K + kk]
            pltpu.make_async_copy(_row_tile(h_ref, r), _row_tile(x_hbm, d), sem.at[kk % 2]).start(priority=kk % 2)
        return carry

    lax.fori_loop(0, tm, body, 0, unroll=8)
    for kk in range(TOP_K):
        pltpu.make_async_copy(h_ref, h_ref, sem.at[kk % 2]).wait()


def _dispatch(grp, h2, dest, x_buf):
    tm = grp.tm
    dblk = dest.reshape(grp.tiles, 1, tm * TOP_K)
    return pl.pallas_call(
        functools.partial(_dispatch_body, tm),
        grid=(grp.tiles,),
        in_specs=[pl.BlockSpec((None, 1, tm * TOP_K), lambda i: (i, 0, 0), memory_space=pltpu.SMEM),
                  pl.BlockSpec((tm * LANE_TILES, LANES), lambda i: (i, 0)), pl.BlockSpec(memory_space=pl.ANY)],
        out_specs=pl.BlockSpec(memory_space=pl.ANY),
        out_shape=jax.ShapeDtypeStruct(x_buf.shape, F32),
        scratch_shapes=[pltpu.SemaphoreType.DMA((2,))],
        input_output_aliases={2: 0},
        compiler_params=pltpu.CompilerParams(dimension_semantics=("arbitrary",), vmem_limit_bytes=VMEM_LIMIT,
                                             has_side_effects=True),
        name="dispatch",
    )(dblk, h2, x_buf)


def _expert_body(be_ref, nu_ref, x_ref, wgu_ref, bgu_ref, wdn_ref, bdn_ref, o_ref, wgu_b, wdn_b):
    i = pl.program_id(0)

    @pl.when(i < nu_ref[0])
    def _():
        prev_e = be_ref[jnp.maximum(i - 1, 0)]

        @pl.when((i == 0) | (be_ref[i] != prev_e))
        def _():
            wgu_b[...] = wgu_ref[...].astype(BF16)
            wdn_b[...] = wdn_ref[...].astype(BF16)

        x = jnp.concatenate(_load_row_tiles(x_ref, EXPERT_ROWS), axis=1).astype(BF16)
        gu = jnp.dot(x, wgu_b[...], preferred_element_type=F32) + bgu_ref[...]
        gate = jnp.minimum(gu[:, :D_FF], SWIGLU_LIMIT)
        lin = jnp.clip(gu[:, D_FF:], -SWIGLU_LIMIT, SWIGLU_LIMIT)
        act = gate * jax.nn.sigmoid(SWIGLU_ALPHA * gate) * (lin + 1.0)
        out = jnp.dot(act.astype(BF16), wdn_b[...], preferred_element_type=F32) + bdn_ref[...]
        _store_row_tiles(o_ref, out, EXPERT_ROWS)

    @pl.when(i >= nu_ref[0])
    def _():
        o_ref[...] = jnp.zeros(o_ref.shape, F32)


def _experts(layer, x_buf, block_e, n_used, w_gu, b_gu, w_dn, b_dn):
    rows = EXPERT_ROWS
    n_blocks = x_buf.shape[0] // (rows * LANE_TILES)
    xblk = (rows * LANE_TILES, LANES)

    def blk(i, be, nu):
        return jnp.minimum(i, nu[0] - 1)

    def wmap(i, be, nu):
        return (layer, be[blk(i, be, nu)], 0, 0)

    grid_spec = pltpu.PrefetchScalarGridSpec(
        num_scalar_prefetch=2,
        grid=(n_blocks,),
        in_specs=[pl.BlockSpec(xblk, lambda i, be, nu: (blk(i, be, nu), 0)),
                  pl.BlockSpec((None, None, D_MODEL, 2 * D_FF), wmap),
                  pl.BlockSpec((None, None, 1, 2 * D_FF), wmap),
                  pl.BlockSpec((None, None, D_FF, D_MODEL), wmap),
                  pl.BlockSpec((None, None, 1, D_MODEL), wmap)],
        out_specs=pl.BlockSpec(xblk, lambda i, be, nu: (i, 0)),
        scratch_shapes=[pltpu.VMEM((D_MODEL, 2 * D_FF), BF16), pltpu.VMEM((D_FF, D_MODEL), BF16)])
    return pl.pallas_call(
        _expert_body,
        grid_spec=grid_spec,
        out_shape=jax.ShapeDtypeStruct(x_buf.shape, F32),
        compiler_params=_params(1),
        name="experts",
    )(block_e, n_used, x_buf, w_gu, b_gu, w_dn, b_dn)


def _combine_body(tm, dcur_ref, dnxt_ref, x1_ref, gt_ref, g2_ref, lg_ref, lb_ref, y_hbm, o_ref, ybuf, sem):
    i = pl.program_id(0)
    slot = i % 2

    def gather(dref, sl):
        def body(r, carry):
            for kk in range(TOP_K):
                d = dref[0, r * TOP_K + kk]
                pltpu.make_async_copy(_row_tile(y_hbm, d), _row_tile(ybuf, r, (sl, kk)),
                                      sem.at[sl, kk % 2]).start(priority=kk % 2)
            return carry
        lax.fori_loop(0, tm, body, 0, unroll=8)

    @pl.when(i == 0)
    def _():
        gather(dcur_ref, 0)

    @pl.when(i + 1 < pl.num_programs(0))
    def _():
        gather(dnxt_ref, 1 - slot)

    for kk in range(TOP_K):
        pltpu.make_async_copy(ybuf.at[slot, kk], ybuf.at[slot, kk], sem.at[slot, kk % 2]).wait()
    gt = gt_ref[...]
    parts = None
    for kk in range(TOP_K):
        gk = jnp.broadcast_to(gt[:, kk:kk + 1], (tm, LANES))
        yk = [gk * t for t in _load_row_tiles(ybuf, tm, (slot, kk))]
        parts = yk if parts is None else [a + b for a, b in zip(parts, yk)]
    ff = jnp.concatenate(parts, axis=1)
    xa = DEEPNORM_ALPHA * x1_ref[...] + g2_ref[...] * ff
    o_ref[...] = _layernorm(xa) * lg_ref[...] + lb_ref[...]


def _combine(grp, x1, gates, dest, y_buf, mod, lg, lb):
    tm = grp.tm
    last = grp.tiles - 1
    dblk = dest.reshape(grp.tiles, 1, tm * TOP_K)
    dspec = lambda f: pl.BlockSpec((None, 1, tm * TOP_K), f, memory_space=pltpu.SMEM)
    return pl.pallas_call(
        functools.partial(_combine_body, tm),
        grid=(grp.tiles,),
        in_specs=[dspec(lambda i: (i, 0, 0)), dspec(lambda i: (jnp.minimum(i + 1, last), 0, 0)),
                  _row_spec(grp, D_MODEL), _row_spec(grp, LANES), _mod_spec(grp, 5),
                  _const_spec(lg.shape), _const_spec(lb.shape), pl.BlockSpec(memory_space=pl.ANY)],
        out_specs=_row_spec(grp, D_MODEL),
        out_shape=jax.ShapeDtypeStruct((grp.rows, D_MODEL), F32),
        scratch_shapes=[pltpu.VMEM((2, TOP_K, tm * LANE_TILES, LANES), F32), pltpu.SemaphoreType.DMA((2, 2))],
        compiler_params=_params(1),
        name="combine_ln",
    )(dblk, dblk, x1, gates, mod, lg, lb, y_buf)


def kernel(x_prompt, x_sample, state_wkv, state_shift, state_conv, c_prompt, c_sample, w_in, tm_mu, w0, w2, a0, a2, g2, v0, v1, v2, k_k, k_a, r_k, lnx_g, lnx_b, conv_w, conv_b, conv_ln_g, conv_ln_b, w_out, ada_w, ada_b, ln1_g, ln1_b, ln2_g, ln2_b, router_w, router_b, w_gu, b_gu, w_dn, b_dn):
    bp, tp, _ = x_prompt.shape
    bs, ts, _ = x_sample.shape
    depth = w_in.shape[0]
    n_p, n_s = bp * tp, bs * ts
    n_tok = n_p + n_s
    gp = _Group(n_p, tp, 1, False)
    gs = _Group(n_s, n_s, bs, True)

    mod_all = _ada(jnp.concatenate([c_prompt, c_sample], axis=0), ada_w, ada_b)

    w_in_b = w_in.astype(BF16)
    w_out_b = w_out.astype(BF16)
    hid = jnp.arange(RWKV_DIM) // HEAD_DIM
    gmat = (hid[:, None] == hid[None, :]).astype(BF16)
    zero_wa = jnp.zeros((depth, DECAY_LORA, RWKV_DIM), F32)
    wwa = jnp.concatenate([jnp.concatenate([w2, zero_wa], axis=2),
                           jnp.concatenate([zero_wa, a2], axis=2)], axis=1).astype(BF16)
    v1p = jnp.pad(v1, ((0, 0), (0, 0), (0, LANES - VRES_LORA))).astype(BF16)
    v2p = jnp.pad(v2, ((0, 0), (0, LANES - VRES_LORA), (0, 0))).astype(BF16)
    rw_p = jnp.pad(router_w, ((0, 0), (0, 0), (0, LANES - N_EXPERTS)))
    rw_hi = rw_p.astype(BF16)
    rw_lo = (rw_p - rw_hi.astype(F32)).astype(BF16)
    rb_p = jnp.pad(router_b, ((0, 0), (0, LANES - N_EXPERTS)), constant_values=-1e30)
    b_gu4 = b_gu.reshape(depth, N_EXPERTS, 1, 2 * D_FF)
    b_dn4 = b_dn.reshape(depth, N_EXPERTS, 1, D_MODEL)

    def row2(a):
        return a.reshape(1, -1)

    x_p = x_prompt.reshape(n_p, D_MODEL)
    x_s = jnp.transpose(x_sample, (1, 0, 2)).reshape(n_s, D_MODEL)
    zeros_wkv = jnp.zeros((bp, RWKV_HEADS, HEAD_DIM, HEAD_DIM), F32)
    zeros_shift = jnp.zeros((bp, 1, SHIFT_DIM), F32)
    zeros_conv = jnp.zeros((bp, CONV_HIST, CONV_DIM), F32)
    wkv_s_t = jnp.transpose(state_wkv, (0, 2, 4, 3, 1))

    vf_p = vf_s = None
    wkv_p_out, wkv_s_out, shift_p_out, shift_s_out, conv_p_out, conv_s_out = [], [], [], [], [], []
    for l in range(depth):
        mod_p = mod_all[l, :bp].reshape(bp, 1, 6 * D_MODEL)
        mod_s = jnp.tile(mod_all[l, bp:], (ts, 1))
        wts = {"mu": row2(tm_mu[l]), "w0": row2(w0[l]), "a0": row2(a0[l]), "k_k": row2(k_k[l]),
               "k_a": row2(k_a[l]), "wwa": wwa[l], "g2": g2[l].astype(BF16),
               "lnx_g": row2(lnx_g[l]), "lnx_b": row2(lnx_b[l]), "r_k": row2(r_k[l]), "w_out": w_out_b[l],
               "ln1_g": row2(ln1_g[l]), "ln1_b": row2(ln1_b[l]), "router_w": rw_hi[l], "router_w_lo": rw_lo[l],
               "router_b": row2(rb_p[l])}
        if l > 0:
            wts.update({"v0": row2(v0[l - 1]), "v1": v1p[l - 1], "v2": v2p[l - 1]})

        per_group = []
        for grp, x, mod, is_p in ((gp, x_p, mod_p, True), (gs, x_s, mod_s, False)):
            zs, zc = _inproj(grp, x, mod, w_in_b[l])
            if is_p:
                shift0, conv0 = zeros_shift, zeros_conv
            else:
                shift0 = state_shift[l][None]
                conv0 = jnp.transpose(state_conv[l], (1, 0, 2)).reshape(1, CONV_HIST * bs, CONV_DIM)
            vfirst = vf_p if is_p else vf_s
            r, lw, k, v, an, bn, g = _prep(grp, zs, shift0, vfirst if l > 0 else None, wts, gmat)
            if l == 0:
                if is_p:
                    vf_p = v
                else:
                    vf_s = v
            if is_p:
                y, s_new = _wkv_chunked(bp, tp, r, lw, k, v, an, bn, zeros_wkv)
                wkv_p_out.append(s_new)
                shift_p_out.append(zs.reshape(bp, tp, SHIFT_DIM)[:, -1])
            else:
                tmaj = lambda a: jnp.transpose(a.reshape(ts, bs, RWKV_DIM), (0, 2, 1))
                y_t, s_new = _wkv_lanes(ts, tmaj(r), tmaj(lw), tmaj(k), tmaj(v), tmaj(an), tmaj(bn), wkv_s_t[l])
                y = jnp.transpose(y_t, (0, 2, 1)).reshape(n_s, RWKV_DIM)
                wkv_s_out.append(jnp.transpose(s_new, (3, 0, 2, 1)))
                shift_s_out.append(zs[(ts - 1) * bs:])
            cv, tail = _conv(grp, zc, conv0, conv_w[l], row2(conv_b[l]), row2(conv_ln_g[l]), row2(conv_ln_b[l]))
            if is_p:
                conv_p_out.append(tail)
            else:
                conv_s_out.append(jnp.transpose(tail.reshape(CONV_HIST, bs, CONV_DIM), (1, 0, 2)))
            off = 0 if is_p else n_p
            x1, h2, ti, gt = _mix(grp, x, y, r, k, v, g, cv, mod, wts, gmat)
            per_group.append((grp, x1, h2, ti, gt, mod, off))

        top_i = jnp.concatenate([pg[3] for pg in per_group], axis=0)
        dest, block_e, n_used, n_rows = _route(top_i, gp.tm)
        if l == 0:
            x_buf = jnp.zeros((n_rows * LANE_TILES, LANES), F32)
        for grp, x1, h2, ti, gt, mod, off in per_group:
            x_buf = _dispatch(grp, h2, dest[off:off + grp.rows], x_buf)
        y_buf = _experts(l, x_buf, block_e, n_used, w_gu, b_gu4, w_dn, b_dn4)
        outs = []
        for grp, x1, h2, ti, gt, mod, off in per_group:
            outs.append(_combine(grp, x1, gt, dest[off:off + grp.rows], y_buf, mod, row2(ln2_g[l]), row2(ln2_b[l])))
        x_p, x_s = outs

    y_prompt = x_p.reshape(bp, tp, D_MODEL)
    y_sample = jnp.transpose(x_s.reshape(ts, bs, D_MODEL), (1, 0, 2))
    return (y_prompt, y_sample, jnp.stack(wkv_p_out), jnp.stack(wkv_s_out), jnp.stack(shift_p_out),
            jnp.stack(shift_s_out), jnp.stack(conv_p_out), jnp.stack(conv_s_out))
```

```python
import functools

import jax
import jax.numpy as jnp
from jax import lax
from jax.experimental import pallas as pl
from jax.experimental.pallas import tpu as pltpu

F32 = jnp.float32
BF16 = jnp.bfloat16

D_MODEL = 1024
DEPTH = 4
HEAD_DIM = 64
RWKV_DIM = 512
RWKV_HEADS = RWKV_DIM // HEAD_DIM
CONV_DIM = 512
CONV_WIDTH = 31
CONV_HIST = CONV_WIDTH - 1
DECAY_LORA = 64
AAA_LORA = 64
GATE_LORA = 128
VRES_LORA = 32
SHIFT_DIM = 3 * RWKV_DIM + DECAY_LORA + AAA_LORA + GATE_LORA
PROJ_DIM = SHIFT_DIM + 2 * CONV_DIM
N_EXPERTS = 32
TOP_K = 4
D_FF = D_MODEL
SWIGLU_LIMIT = 7.0
SWIGLU_ALPHA = 1.702
LN_EPS = 1e-5
GN_EPS = 64e-5
DEEPNORM_ALPHA = (2 * DEPTH) ** 0.25

LANES = 128
SUBLANES = 8
ROW_TILE = 512
CHUNK = 64
WKV_BATCHES_PER_STEP = 2
EXPERT_ROWS = 256
VMEM_LIMIT = 56 * 1024 * 1024


def _params(n_axes=1, vmem=VMEM_LIMIT):
    return pltpu.CompilerParams(dimension_semantics=("arbitrary",) * n_axes, vmem_limit_bytes=vmem)


def _round_up(x, m):
    return (x + m - 1) // m * m


def _bdot(a, b):
    return jnp.dot(a.astype(BF16), b.astype(BF16), preferred_element_type=F32)


def _split3(x):
    hi = x.astype(BF16)
    r1 = x - hi.astype(F32)
    mid = r1.astype(BF16)
    lo = (r1 - mid.astype(F32)).astype(BF16)
    return hi, mid, lo


def _headsum(x, g):
    return jnp.dot(x.astype(BF16), g, preferred_element_type=F32)


LANE_TILES = D_MODEL // LANES
assert LANE_TILES == SUBLANES


def _store_row_tiles(ref, x, rows, lead=()):
    for c in range(LANE_TILES):
        ref[lead + (pl.ds(c, rows, stride=LANE_TILES), slice(None))] = x[:, c * LANES:(c + 1) * LANES]


def _load_row_tiles(ref, rows, lead=()):
    return [ref[lead + (pl.ds(c, rows, stride=LANE_TILES), slice(None))] for c in range(LANE_TILES)]


def _row_tile(ref, r, lead=()):
    return ref.at[lead + (pl.ds(pl.multiple_of(r * LANE_TILES, LANE_TILES), LANE_TILES), slice(None))]


def _layernorm(x):
    mu = jnp.mean(x, axis=-1, keepdims=True)
    xc = x - mu
    var = jnp.mean(xc * xc, axis=-1, keepdims=True)
    return xc * lax.rsqrt(var + LN_EPS)


def _ada_body(c_ref, w_ref, b_ref, o_ref):
    c = c_ref[...]
    cs = c * jax.nn.sigmoid(c)
    o_ref[...] = _bdot(cs, w_ref[...]) + b_ref[...]


def _ada(c_all, ada_w, ada_b):
    depth, _, width = ada_w.shape
    nb = c_all.shape[0]
    tn = 1536
    return pl.pallas_call(
        _ada_body,
        grid=(depth, width // tn),
        in_specs=[pl.BlockSpec((nb, D_MODEL), lambda l, j: (0, 0)),
                  pl.BlockSpec((None, D_MODEL, tn), lambda l, j: (l, 0, j)),
                  pl.BlockSpec((None, 1, tn), lambda l, j: (l, 0, j))],
        out_specs=pl.BlockSpec((None, nb, tn), lambda l, j: (l, 0, j)),
        out_shape=jax.ShapeDtypeStruct((depth, nb, width), F32),
        compiler_params=_params(2),
        name="ada_mod",
    )(c_all, ada_w, ada_b.reshape(depth, 1, width))


class _Group:
    def __init__(self, rows, seg_len, stride, mod_per_row):
        self.rows = rows
        self.seg_len = seg_len
        self.stride = stride
        self.mod_per_row = mod_per_row
        self.tm = min(ROW_TILE, rows)
        assert rows % self.tm == 0 and seg_len % self.tm == 0
        self.tiles = rows // self.tm
        self.tiles_per_seg = seg_len // self.tm
        self.nseg = rows // seg_len


def _mod_spec(grp, which):
    if grp.mod_per_row:
        return pl.BlockSpec((grp.tm, D_MODEL), lambda i: (i, which))
    tps = grp.tiles_per_seg
    return pl.BlockSpec((None, 1, D_MODEL), lambda i: (i // tps, 0, which))


def _row_spec(grp, width):
    return pl.BlockSpec((grp.tm, width), lambda i: (i, 0))


def _const_spec(shape):
    nd = len(shape)
    return pl.BlockSpec(shape, lambda i: (0,) * nd)


def _prep_body(has_vres, grp, halo, *refs):
    it = iter(refs)
    x_ref, sc_ref, sh_ref, win_ref, sh0_ref = (next(it) for _ in range(5))
    vf_ref = next(it) if has_vres else None
    mu_ref, w0_ref, a0_ref, kk_ref, ka_ref, wwa_ref, g2_ref = (next(it) for _ in range(7))
    if has_vres:
        v0_ref, v1_ref, v2_ref = next(it), next(it), next(it)
    gmat_ref = next(it)
    r_o, lw_o, k_o, v_o, an_o, bn_o, g_o, zc_o, shl_o = (next(it) for _ in range(9))
    ext = next(it)

    tm, s = grp.tm, grp.stride
    j = pl.program_id(0) % grp.tiles_per_seg

    @pl.when(j == 0)
    def _():
        ext[pl.ds(halo - s, s), :] = sh0_ref[...]

    h = x_ref[...] * (1.0 + sc_ref[...]) + sh_ref[...]
    z = jnp.dot(h.astype(BF16), win_ref[...], preferred_element_type=F32)
    zc_o[...] = z[:, SHIFT_DIM:]
    zs = z[:, :SHIFT_DIM]
    ext[pl.ds(halo, tm), :] = zs
    prev = ext[pl.ds(halo - s, tm), :]
    shl_o[...] = ext[pl.ds(halo + tm - s, s), :]
    if grp.tiles_per_seg > 1:
        ext[pl.ds(0, halo), :] = ext[pl.ds(tm, halo), :]

    zm = zs + (prev - zs) * mu_ref[...]
    i1, i2, i3 = RWKV_DIM, 2 * RWKV_DIM, 3 * RWKV_DIM
    i5 = i3 + DECAY_LORA + AAA_LORA
    r, k, v = zm[:, :i1], zm[:, i1:i2], zm[:, i2:i3]
    zwa, zg = zm[:, i3:i5], zm[:, i5:]
    lane = lax.broadcasted_iota(jnp.int32, zwa.shape, 1)
    xwa = jnp.where(lane < DECAY_LORA, jnp.tanh(zwa), zwa)
    wa = _bdot(xwa, wwa_ref[...])
    w_in = w0_ref[...] + wa[:, :RWKV_DIM]
    neg = -w_in
    softplus = jnp.maximum(neg, 0.0) + jnp.log(1.0 + jnp.exp(-jnp.abs(neg)))
    w_log = -softplus - 0.5
    lw = -jnp.exp(w_log)
    a = jax.nn.sigmoid(a0_ref[...] + wa[:, RWKV_DIM:])
    g = _bdot(jax.nn.sigmoid(zg), g2_ref[...])
    if has_vres:
        vl = _bdot(_bdot(v, v1_ref[...]), v2_ref[...])
        vg = jax.nn.sigmoid(v0_ref[...] + vl)
        v = v + (vf_ref[...] - v) * vg
    kk = k * kk_ref[...]
    ss = _headsum(kk * kk, gmat_ref[...])
    kk = kk * lax.rsqrt(jnp.maximum(ss, 1e-24))
    k = k * (1.0 + (a - 1.0) * ka_ref[...])
    r_o[...] = r
    lw_o[...] = lw
    k_o[...] = k
    v_o[...] = v
    an_o[...] = -kk
    bn_o[...] = kk * a
    g_o[...] = g


def _prep(grp, x, mod, w_in_b, shift0, vfirst, wts, gmat):
    has_vres = vfirst is not None
    halo = _round_up(grp.stride, SUBLANES)
    s = grp.stride
    tps = grp.tiles_per_seg
    seg_spec = pl.BlockSpec((None, s, SHIFT_DIM), lambda i: (i // tps, 0, 0))
    in_specs = [_row_spec(grp, D_MODEL), _mod_spec(grp, 1), _mod_spec(grp, 0),
                _const_spec((D_MODEL, PROJ_DIM)), seg_spec]
    args = [x, mod, mod, w_in_b, shift0]
    if has_vres:
        in_specs.append(_row_spec(grp, RWKV_DIM))
        args.append(vfirst)
    names = ["mu", "w0", "a0", "k_k", "k_a", "wwa", "g2"] + (["v0", "v1", "v2"] if has_vres else [])
    for nm in names:
        in_specs.append(_const_spec(wts[nm].shape))
        args.append(wts[nm])
    in_specs.append(_const_spec(gmat.shape))
    args.append(gmat)
    out = jax.ShapeDtypeStruct((grp.rows, RWKV_DIM), F32)
    return pl.pallas_call(
        functools.partial(_prep_body, has_vres, grp, halo),
        grid=(grp.tiles,),
        in_specs=in_specs,
        out_specs=[_row_spec(grp, RWKV_DIM)] * 7 + [_row_spec(grp, 2 * CONV_DIM), seg_spec],
        out_shape=[out] * 7 + [jax.ShapeDtypeStruct((grp.rows, 2 * CONV_DIM), F32),
                               jax.ShapeDtypeStruct((grp.nseg, s, SHIFT_DIM), F32)],
        scratch_shapes=[pltpu.VMEM((halo + grp.tm, SHIFT_DIM), F32)],
        compiler_params=_params(1),
        name="inproj_prep",
    )(*args)


def _nt(a, b):
    return lax.dot_general(a.astype(BF16), b.astype(BF16), (((1,), (1,)), ((), ())),
                           preferred_element_type=F32)


def _tn(a, b):
    return lax.dot_general(a.astype(BF16), b.astype(BF16), (((0,), (0,)), ((), ())),
                           preferred_element_type=F32)


def _wkv_chunk_body(nchunks, nb, r_ref, lw_ref, k_ref, v_ref, a_ref, b_ref, s0_ref, y_ref, so_ref, st):
    c = pl.program_id(1)
    L, N = CHUNK, HEAD_DIM

    @pl.when(c == 0)
    def _():
        st[...] = s0_ref[...]

    P2, W2 = 2 * L, 2 * N
    assert L == N and W2 == LANES
    trow = lax.broadcasted_iota(jnp.int32, (L, L), 0)
    tcol = lax.broadcasted_iota(jnp.int32, (L, L), 1)
    tri = jnp.where(trow >= tcol, 1.0, 0.0).astype(BF16)
    row = lax.broadcasted_iota(jnp.int32, (P2, P2), 0)
    col = lax.broadcasted_iota(jnp.int32, (P2, P2), 1)
    lg = L.bit_length() - 1
    same_head = (row >> lg) == (col >> lg)
    strict = same_head & (row > col)
    incl = same_head & (row >= col)
    eye = row == col
    sh = 3
    diag_blk = (row >> sh) == (col >> sh)
    off_masks = []
    while (1 << sh) < L:
        off_masks.append(((row >> sh) == (col >> sh) + 1) & ((row >> (sh + 1)) == (col >> (sh + 1))))
        sh += 1

    lane_lo = lax.broadcasted_iota(jnp.int32, (L, W2), 1) < N
    npair = RWKV_HEADS // 2

    def stack(x, p):
        xp = x[:, p * W2:(p + 1) * W2]
        return jnp.concatenate([jnp.where(lane_lo, xp, 0.0), jnp.where(lane_lo, 0.0, xp)], axis=0)

    at, rt, bt, kt, bh, kh, vv, p_last = [], [], [], [], [], [], [], []
    for bi in range(nb):
        lw = lw_ref[bi]
        h3 = _split3(lw)
        cl = (jnp.dot(tri, h3[0], preferred_element_type=F32) + jnp.dot(tri, h3[1], preferred_element_type=F32)
              + jnp.dot(tri, h3[2], preferred_element_type=F32))
        cl_last = cl[L - 1:L, :]
        ecl = jnp.exp(cl)
        encl = jnp.exp(-cl)
        dl = jnp.exp(cl_last - cl)
        pl_row = jnp.exp(cl_last)
        an, bn, kx, rx, vx = a_ref[bi], b_ref[bi], k_ref[bi], r_ref[bi], v_ref[bi]
        at_all, rt_all = an * jnp.exp(cl - lw), rx * ecl
        bt_all, kt_all, bh_all, kh_all = bn * encl, kx * encl, bn * dl, kx * dl
        for p in range(npair):
            at.append(stack(at_all, p).astype(BF16))
            rt.append(stack(rt_all, p))
            bt.append(stack(bt_all, p).astype(BF16))
            kt.append(stack(kt_all, p).astype(BF16))
            bh.append(stack(bh_all, p).astype(BF16))
            kh.append(stack(kh_all, p).astype(BF16))
            vv.append(stack(vx, p).astype(BF16))
            p_last.append(pl_row[:, p * W2:(p + 1) * W2])
    pairs = range(nb * npair)

    ar = [jnp.concatenate([at[p], rt[p].astype(BF16)], axis=0) for p in pairs]
    xb = [_nt(ar[p], bt[p]) for p in pairs]
    xk = [_nt(ar[p], kt[p]) for p in pairs]
    a_ab = [jnp.where(strict, xb[p][:P2], 0.0) for p in pairs]
    a_ak = [jnp.where(strict, xk[p][:P2], 0.0) for p in pairs]
    a_rb = [jnp.where(incl, xb[p][P2:], 0.0).astype(BF16) for p in pairs]
    a_rk = [jnp.where(incl, xk[p][P2:], 0.0) for p in pairs]
    pw = [jnp.where(diag_blk, a_ab[p], 0.0) for p in pairs]
    tm = [jnp.where(eye, 1.0, 0.0) + pw[p] for p in pairs]
    for _ in range(2):
        pw = [_bdot(pw[p], pw[p]) for p in pairs]
        tm = [tm[p] + _bdot(tm[p], pw[p]) for p in pairs]
    for off in off_masks:
        t1 = [_bdot(jnp.where(off, a_ab[p], 0.0), tm[p]) for p in pairs]
        tm = [tm[p] + _bdot(tm[p], t1[p]) for p in pairs]
    akv = [_bdot(a_ak[p], vv[p]).astype(BF16) for p in pairs]
    wu = [_bdot(tm[p], jnp.concatenate([at[p], akv[p]], axis=1)) for p in pairs]
    qy = [jnp.dot(a_rb[p], wu[p].astype(BF16), preferred_element_type=F32) for p in pairs]
    rkv = [_bdot(a_rk[p], vv[p]) for p in pairs]
    ys = []
    for p in pairs:
        bi, pp = divmod(p, npair)
        s_p = st[bi, pp]
        w, u0 = wu[p][:, :W2], wu[p][:, W2:]
        q = rt[p] + qy[p][:, :W2]
        ybd = _nt(q, s_p) + qy[p][:, W2:] + rkv[p]
        ys.append(ybd[:L] + ybd[L:])
        cm = _tn(u0, bh[p]) + _tn(vv[p], kh[p])
        st[bi, pp] = s_p * p_last[p] + _bdot(s_p, _tn(w, bh[p])) + cm
    for bi in range(nb):
        y_ref[bi] = jnp.concatenate(ys[bi * npair:(bi + 1) * npair], axis=1)

    @pl.when(c == nchunks - 1)
    def _():
        so_ref[...] = st[...]


def _wkv_chunked(nbatch, tlen, r, lw, k, v, an, bn, s0):
    nchunks = tlen // CHUNK
    npair, w2 = RWKV_HEADS // 2, 2 * HEAD_DIM
    s5 = s0.reshape(nbatch, npair, 2, HEAD_DIM, HEAD_DIM)
    zero = jnp.zeros_like(s5[:, :, 0])
    s_bd = jnp.concatenate([jnp.concatenate([s5[:, :, 0], zero], axis=-1),
                            jnp.concatenate([zero, s5[:, :, 1]], axis=-1)], axis=-2)
    nb = WKV_BATCHES_PER_STEP if nbatch % WKV_BATCHES_PER_STEP == 0 else 1
    tok = pl.BlockSpec((nb, CHUNK, RWKV_DIM), lambda b, c: (b, c, 0))
    stt = pl.BlockSpec((nb, npair, w2, w2), lambda b, c: (b, 0, 0, 0))
    seq = lambda t: t.reshape(nbatch, tlen, RWKV_DIM)
    y, so = pl.pallas_call(
        functools.partial(_wkv_chunk_body, nchunks, nb),
        grid=(nbatch // nb, nchunks),
        in_specs=[tok] * 6 + [stt],
        out_specs=[tok, stt],
        out_shape=[jax.ShapeDtypeStruct((nbatch, tlen, RWKV_DIM), F32),
                   jax.ShapeDtypeStruct((nbatch, npair, w2, w2), F32)],
        scratch_shapes=[pltpu.VMEM((nb, npair, w2, w2), F32)],
        compiler_params=_params(2),
        name="wkv_chunked",
    )(seq(r), seq(lw), seq(k), seq(v), seq(an), seq(bn), s_bd)
    s_new = jnp.stack([so[:, :, :HEAD_DIM, :HEAD_DIM], so[:, :, HEAD_DIM:, HEAD_DIM:]], axis=2)
    return y.reshape(nbatch * tlen, RWKV_DIM), s_new.reshape(nbatch, RWKV_HEADS, HEAD_DIM, HEAD_DIM)


def _wkv_lane_body(tlen, r_ref, lw_ref, k_ref, v_ref, a_ref, b_ref, s_ref, y_ref, so_ref):
    N = HEAD_DIM
    nb = s_ref.shape[-1]

    def row(ref, t, j):
        return jnp.broadcast_to(ref[t, pl.ds(j, 1), :], (N, nb))

    def first(j, acc):
        return acc + s_ref[j] * row(a_ref, 0, j)

    sa = lax.fori_loop(0, N, first, jnp.zeros((N, nb), F32))
    for t in range(tlen):
        v_t = v_ref[t]
        src = s_ref if t == 0 else so_ref

        def body(j, carry, t=t, v_t=v_t, src=src, sa=sa):
            y, sa_next = carry
            sj = src[j] * jnp.exp(row(lw_ref, t, j)) + sa * row(b_ref, t, j) + v_t * row(k_ref, t, j)
            so_ref[j] = sj
            y = y + sj * row(r_ref, t, j)
            if t + 1 < tlen:
                sa_next = sa_next + sj * row(a_ref, t + 1, j)
            return y, sa_next

        zero = jnp.zeros((N, nb), F32)
        y, sa = lax.fori_loop(0, N, body, (zero, zero))
        y_ref[t] = y


def _wkv_lanes(tlen, r, lw, k, v, an, bn, s_t):
    nb = r.shape[-1]
    tok = pl.BlockSpec((tlen, HEAD_DIM, nb), lambda h: (0, h, 0))
    stt = pl.BlockSpec((None, HEAD_DIM, HEAD_DIM, nb), lambda h: (h, 0, 0, 0))
    return pl.pallas_call(
        functools.partial(_wkv_lane_body, tlen),
        grid=(RWKV_HEADS,),
        in_specs=[tok] * 6 + [stt],
        out_specs=[tok, stt],
        out_shape=[jax.ShapeDtypeStruct((tlen, RWKV_DIM, nb), F32),
                   jax.ShapeDtypeStruct((RWKV_HEADS, HEAD_DIM, HEAD_DIM, nb), F32)],
        compiler_params=_params(1),
        name="wkv_lanes",
    )(r, lw, k, v, an, bn, s_t)


def _conv_body(grp, halo, zc_ref, c0_ref, cw_ref, cb_ref, lg_ref, lb_ref, cv_ref, tail_ref, ext, win):
    tm, s = grp.tm, grp.stride
    hist = CONV_HIST * s
    j = pl.program_id(0) % grp.tiles_per_seg

    @pl.when(j == 0)
    def _():
        ext[pl.ds(halo - hist, hist), :] = c0_ref[...]

    zc = zc_ref[...]
    u = zc[:, :CONV_DIM] * jax.nn.sigmoid(zc[:, CONV_DIM:])
    ext[pl.ds(halo, tm), :] = u
    acc = jnp.broadcast_to(cb_ref[...], (tm, CONV_DIM))
    offs = [halo - hist + w * s for w in range(CONV_WIDTH)]
    for phase in range(SUBLANES):
        taps = [w for w in range(CONV_WIDTH) if offs[w] % SUBLANES == phase]
        if not taps:
            continue
        if s % SUBLANES == 0:
            for w in taps:
                acc = acc + cw_ref[pl.ds(w, 1), :] * ext[pl.ds(offs[w], tm), :]
            continue
        lo = offs[taps[0]]
        span = tm + offs[taps[-1]] - lo
        win[pl.ds(0, span), :] = ext[pl.ds(lo, span), :]
        for w in taps:
            acc = acc + cw_ref[pl.ds(w, 1), :] * win[pl.ds(offs[w] - lo, tm), :]
    y = _layernorm(acc) * lg_ref[...] + lb_ref[...]
    cv_ref[...] = y * jax.nn.sigmoid(y)
    tail_ref[...] = ext[pl.ds(halo + tm - hist, hist), :]
    if grp.tiles_per_seg > 1:
        ext[pl.ds(0, halo), :] = ext[pl.ds(tm, halo), :]


def _conv(grp, zc, conv0, cw, cb, lg, lb):
    hist = CONV_HIST * grp.stride
    halo = _round_up(hist, SUBLANES)
    tps = grp.tiles_per_seg
    hist_spec = pl.BlockSpec((None, hist, CONV_DIM), lambda i: (i // tps, 0, 0))
    return pl.pallas_call(
        functools.partial(_conv_body, grp, halo),
        grid=(grp.tiles,),
        in_specs=[_row_spec(grp, 2 * CONV_DIM), hist_spec,
                  _const_spec(cw.shape), _const_spec(cb.shape), _const_spec(lg.shape), _const_spec(lb.shape)],
        out_specs=[_row_spec(grp, CONV_DIM), hist_spec],
        out_shape=[jax.ShapeDtypeStruct((grp.rows, CONV_DIM), F32),
                   jax.ShapeDtypeStruct((grp.nseg, hist, CONV_DIM), F32)],
        scratch_shapes=[pltpu.VMEM((halo + grp.tm, CONV_DIM), F32),
                        pltpu.VMEM((grp.tm + _round_up(CONV_HIST, SUBLANES), CONV_DIM), F32)],
        compiler_params=_params(1),
        name="conv_group",
    )(zc, conv0, cw, cb, lg, lb)


def _mix_body(x_ref, y_ref, r_ref, k_ref, v_ref, g_ref, cv_ref, g1_ref, sc2_ref, sh2_ref,
              lnxg_ref, lnxb_ref, rk_ref, wout_ref, l1g_ref, l1b_ref, rw_ref, rwl_ref, rb_ref, gmat_ref,
              x1_ref, h2_ref, ti_ref, gt_ref):
    gm = gmat_ref[...]
    y = y_ref[...]
    inv = 1.0 / HEAD_DIM
    ym = _headsum(y, gm) * inv
    yc = y - ym
    yv = _headsum(yc * yc, gm) * inv
    yn = yc * lax.rsqrt(yv + GN_EPS) * lnxg_ref[...] + lnxb_ref[...]
    bonus = _headsum(r_ref[...] * k_ref[...] * rk_ref[...], gm) * v_ref[...]
    yo = (yn + bonus) * g_ref[...]
    mix = (jnp.dot(yo.astype(BF16), wout_ref[pl.ds(0, RWKV_DIM), :], preferred_element_type=F32)
           + jnp.dot(cv_ref[...].astype(BF16), wout_ref[pl.ds(RWKV_DIM, CONV_DIM), :],
                     preferred_element_type=F32))
    xa = DEEPNORM_ALPHA * x_ref[...] + g1_ref[...] * mix
    x1 = _layernorm(xa) * l1g_ref[...] + l1b_ref[...]
    x1_ref[...] = x1
    h2 = x1 * (1.0 + sc2_ref[...]) + sh2_ref[...]
    _store_row_tiles(h2_ref, h2, h2.shape[0])
    h_hi = h2.astype(BF16)
    h_lo = (h2 - h_hi.astype(F32)).astype(BF16)
    logits = (jnp.dot(h_hi, rw_ref[...], preferred_element_type=F32)
              + jnp.dot(h_hi, rwl_ref[...], preferred_element_type=F32)
              + jnp.dot(h_lo, rw_ref[...], preferred_element_type=F32)) + rb_ref[...]
    lane = lax.broadcasted_iota(jnp.int32, logits.shape, 1).astype(F32)
    cur = logits
    vals, idxs = [], []
    for _ in range(TOP_K):
        m = jnp.max(cur, axis=-1, keepdims=True)
        idx = jnp.min(jnp.where(cur == m, lane, float(LANES)), axis=-1, keepdims=True)
        vals.append(m)
        idxs.append(idx)
        cur = jnp.where(lane == idx, -jnp.inf, cur)
    es = [jnp.exp(vv - vals[0]) for vv in vals]
    den = es[0] + es[1] + es[2] + es[3]
    ti = jnp.zeros_like(logits)
    gt = jnp.zeros_like(logits)
    for kk in range(TOP_K):
        ti = jnp.where(lane == float(kk), idxs[kk], ti)
        gt = jnp.where(lane == float(kk), es[kk] / den, gt)
    ti_ref[...] = ti.astype(jnp.int32)
    gt_ref[...] = gt


def _mix(grp, x, y, r, k, v, g, cv, mod, wts, gmat):
    half = _row_spec(grp, RWKV_DIM)
    names = ["lnx_g", "lnx_b", "r_k", "w_out", "ln1_g", "ln1_b", "router_w", "router_w_lo", "router_b"]
    in_specs = ([_row_spec(grp, D_MODEL)] + [half] * 6 + [_mod_spec(grp, 2), _mod_spec(grp, 4), _mod_spec(grp, 3)]
                + [_const_spec(wts[nm].shape) for nm in names] + [_const_spec(gmat.shape)])
    args = [x, y, r, k, v, g, cv, mod, mod, mod] + [wts[nm] for nm in names] + [gmat]
    return pl.pallas_call(
        _mix_body,
        grid=(grp.tiles,),
        in_specs=in_specs,
        out_specs=[_row_spec(grp, D_MODEL), pl.BlockSpec((grp.tm * LANE_TILES, LANES), lambda i: (i, 0)),
                   _row_spec(grp, LANES), _row_spec(grp, LANES)],
        out_shape=[jax.ShapeDtypeStruct((grp.rows, D_MODEL), F32),
                   jax.ShapeDtypeStruct((grp.rows * LANE_TILES, LANES), F32),
                   jax.ShapeDtypeStruct((grp.rows, LANES), jnp.int32),
                   jax.ShapeDtypeStruct((grp.rows, LANES), F32)],
        compiler_params=_params(1),
        name="mix_out",
    )(*args)


def _route_body(ti_ref, dest_ref, cnt_ref, carry, pstart):
    phase = pl.program_id(0)
    i = pl.program_id(1)
    ti = ti_ref[...]
    tm = ti.shape[0]
    lane = lax.broadcasted_iota(jnp.int32, (tm, LANES), 1)
    sel = [lane == ti[:, kk:kk + 1] for kk in range(TOP_K)]
    onehot = jnp.zeros((tm, LANES), F32)
    for kk in range(TOP_K):
        onehot = onehot + jnp.where(sel[kk], 1.0, 0.0)

    @pl.when((phase == 0) & (i == 0))
    def _():
        carry[...] = jnp.zeros(carry.shape, F32)

    @pl.when((phase == 1) & (i == 0))
    def _():
        cnt = carry[...]
        cnt_ref[...] = cnt
        padded = jnp.floor((cnt + (EXPERT_ROWS - 1)) * (1.0 / EXPERT_ROWS)) * EXPERT_ROWS
        er = lax.broadcasted_iota(jnp.int32, (LANES, LANES), 0)
        ec = lax.broadcasted_iota(jnp.int32, (LANES, LANES), 1)
        earlier = jnp.where(er < ec, 1.0, 0.0).astype(BF16)
        h3 = _split3(jnp.broadcast_to(padded, (SUBLANES, LANES)))
        acc = (jnp.dot(h3[0], earlier, preferred_element_type=F32)
               + jnp.dot(h3[1], earlier, preferred_element_type=F32)
               + jnp.dot(h3[2], earlier, preferred_element_type=F32))
        pstart[...] = acc[0:1, :]
        carry[...] = jnp.zeros(carry.shape, F32)

    @pl.when(phase == 1)
    def _():
        rr = lax.broadcasted_iota(jnp.int32, (tm, tm), 0)
        cc = lax.broadcasted_iota(jnp.int32, (tm, tm), 1)
        before = jnp.where(rr > cc, 1.0, 0.0).astype(BF16)
        cum = jnp.dot(before, onehot.astype(BF16), preferred_element_type=F32) + (carry[...] + pstart[...])
        dest = jnp.zeros((tm, LANES), F32)
        for kk in range(TOP_K):
            pk = jnp.sum(jnp.where(sel[kk], cum, 0.0), axis=-1, keepdims=True)
            dest = jnp.where(lane == kk, pk, dest)
        dest_ref[...] = dest.astype(jnp.int32)

    carry[...] = carry[...] + jnp.sum(onehot, axis=0, keepdims=True)


def _route(top_i, tm):
    n_tok = top_i.shape[0]
    rows = EXPERT_ROWS
    dest, cnt = pl.pallas_call(
        _route_body,
        grid=(2, n_tok // tm),
        in_specs=[pl.BlockSpec((tm, LANES), lambda ph, i: (i, 0))],
        out_specs=[pl.BlockSpec((tm, LANES), lambda ph, i: (i * ph, 0)),
                   pl.BlockSpec((1, LANES), lambda ph, i: (0, 0))],
        out_shape=[jax.ShapeDtypeStruct((n_tok, LANES), jnp.int32), jax.ShapeDtypeStruct((1, LANES), F32)],
        scratch_shapes=[pltpu.VMEM((1, LANES), F32), pltpu.VMEM((1, LANES), F32)],
        compiler_params=_params(2),
        name="route_rank",
    )(top_i)
    n_blocks = -(-(n_tok * TOP_K) // rows) + N_EXPERTS
    counts = cnt[0, :N_EXPERTS].astype(jnp.int32)
    padded_end = jnp.cumsum((counts + rows - 1) // rows * rows)
    block_start = jnp.arange(n_blocks, dtype=jnp.int32) * rows
    block_e = jnp.sum((padded_end[None, :] <= block_start[:, None]).astype(jnp.int32), axis=1)
    block_e = jnp.minimum(block_e, N_EXPERTS - 1)
    n_used = (padded_end[-1] // rows).astype(jnp.int32).reshape(1)
    return dest[:, :TOP_K], block_e, n_used, n_blocks * rows


def _dispatch_body(tm, dst_ref, h_ref, xin_ref, x_hbm, sem):
    del xin_ref

    def body(r, carry):
        for kk in range(TOP_K):
            d = dst_ref[0, r * TOP_K + kk]
            pltpu.make_async_copy(_row_tile(h_ref, r), _row_tile(x_hbm, d), sem.at[kk % 2]).start(priority=kk % 2)
        return carry

    lax.fori_loop(0, tm, body, 0, unroll=8)
    for kk in range(TOP_K):
        pltpu.make_async_copy(h_ref, h_ref, sem.at[kk % 2]).wait()


def _dispatch(grp, h2, dest, x_buf):
    tm = grp.tm
    dblk = dest.reshape(grp.tiles, 1, tm * TOP_K)
    return pl.pallas_call(
        functools.partial(_dispatch_body, tm),
        grid=(grp.tiles,),
        in_specs=[pl.BlockSpec((None, 1, tm * TOP_K), lambda i: (i, 0, 0), memory_space=pltpu.SMEM),
                  pl.BlockSpec((tm * LANE_TILES, LANES), lambda i: (i, 0)), pl.BlockSpec(memory_space=pl.ANY)],
        out_specs=pl.BlockSpec(memory_space=pl.ANY),
        out_shape=jax.ShapeDtypeStruct(x_buf.shape, F32),
        scratch_shapes=[pltpu.SemaphoreType.DMA((2,))],
        input_output_aliases={2: 0},
        compiler_params=pltpu.CompilerParams(dimension_semantics=("arbitrary",), vmem_limit_bytes=VMEM_LIMIT,
                                             has_side_effects=True),
        name="dispatch",
    )(dblk, h2, x_buf)


def _expert_body(be_ref, nu_ref, x_ref, wgu_ref, bgu_ref, wdn_ref, bdn_ref, o_ref, wgu_b, wdn_b):
    i = pl.program_id(0)

    @pl.when(i < nu_ref[0])
    def _():
        prev_e = be_ref[jnp.maximum(i - 1, 0)]

        @pl.when((i == 0) | (be_ref[i] != prev_e))
        def _():
            wgu_b[...] = wgu_ref[...].astype(BF16)
            wdn_b[...] = wdn_ref[...].astype(BF16)

        x = jnp.concatenate(_load_row_tiles(x_ref, EXPERT_ROWS), axis=1).astype(BF16)
        gu = jnp.dot(x, wgu_b[...], preferred_element_type=F32) + bgu_ref[...]
        gate = jnp.minimum(gu[:, :D_FF], SWIGLU_LIMIT)
        lin = jnp.clip(gu[:, D_FF:], -SWIGLU_LIMIT, SWIGLU_LIMIT)
        act = gate * jax.nn.sigmoid(SWIGLU_ALPHA * gate) * (lin + 1.0)
        out = jnp.dot(act.astype(BF16), wdn_b[...], preferred_element_type=F32) + bdn_ref[...]
        _store_row_tiles(o_ref, out, EXPERT_ROWS)

    @pl.when(i >= nu_ref[0])
    def _():
        o_ref[...] = jnp.zeros(o_ref.shape, F32)


def _experts(layer, x_buf, block_e, n_used, w_gu, b_gu, w_dn, b_dn):
    rows = EXPERT_ROWS
    n_blocks = x_buf.shape[0] // (rows * LANE_TILES)
    xblk = (rows * LANE_TILES, LANES)

    def blk(i, be, nu):
        return jnp.minimum(i, nu[0] - 1)

    def wmap(i, be, nu):
        return (layer, be[blk(i, be, nu)], 0, 0)

    grid_spec = pltpu.PrefetchScalarGridSpec(
        num_scalar_prefetch=2,
        grid=(n_blocks,),
        in_specs=[pl.BlockSpec(xblk, lambda i, be, nu: (blk(i, be, nu), 0)),
                  pl.BlockSpec((None, None, D_MODEL, 2 * D_FF), wmap),
                  pl.BlockSpec((None, None, 1, 2 * D_FF), wmap),
                  pl.BlockSpec((None, None, D_FF, D_MODEL), wmap),
                  pl.BlockSpec((None, None, 1, D_MODEL), wmap)],
        out_specs=pl.BlockSpec(xblk, lambda i, be, nu: (i, 0)),
        scratch_shapes=[pltpu.VMEM((D_MODEL, 2 * D_FF), BF16), pltpu.VMEM((D_FF, D_MODEL), BF16)])
    return pl.pallas_call(
        _expert_body,
        grid_spec=grid_spec,
        out_shape=jax.ShapeDtypeStruct(x_buf.shape, F32),
        compiler_params=_params(1),
        name="experts",
    )(block_e, n_used, x_buf, w_gu, b_gu, w_dn, b_dn)


def _combine_body(tm, dcur_ref, dnxt_ref, x1_ref, gt_ref, g2_ref, lg_ref, lb_ref, y_hbm, o_ref, ybuf, sem):
    i = pl.program_id(0)
    slot = i % 2

    def gather(dref, sl):
        def body(r, carry):
            for kk in range(TOP_K):
                d = dref[0, r * TOP_K + kk]
                pltpu.make_async_copy(_row_tile(y_hbm, d), _row_tile(ybuf, r, (sl, kk)),
                                      sem.at[sl, kk % 2]).start(priority=kk % 2)
            return carry
        lax.fori_loop(0, tm, body, 0, unroll=8)

    @pl.when(i == 0)
    def _():
        gather(dcur_ref, 0)

    @pl.when(i + 1 < pl.num_programs(0))
    def _():
        gather(dnxt_ref, 1 - slot)

    for kk in range(TOP_K):
        pltpu.make_async_copy(ybuf.at[slot, kk], ybuf.at[slot, kk], sem.at[slot, kk % 2]).wait()
    gt = gt_ref[...]
    parts = None
    for kk in range(TOP_K):
        gk = jnp.broadcast_to(gt[:, kk:kk + 1], (tm, LANES))
        yk = [gk * t for t in _load_row_tiles(ybuf, tm, (slot, kk))]
        parts = yk if parts is None else [a + b for a, b in zip(parts, yk)]
    ff = jnp.concatenate(parts, axis=1)
    xa = DEEPNORM_ALPHA * x1_ref[...] + g2_ref[...] * ff
    o_ref[...] = _layernorm(xa) * lg_ref[...] + lb_ref[...]


def _combine(grp, x1, gates, dest, y_buf, mod, lg, lb):
    tm = grp.tm
    last = grp.tiles - 1
    dblk = dest.reshape(grp.tiles, 1, tm * TOP_K)
    dspec = lambda f: pl.BlockSpec((None, 1, tm * TOP_K), f, memory_space=pltpu.SMEM)
    return pl.pallas_call(
        functools.partial(_combine_body, tm),
        grid=(grp.tiles,),
        in_specs=[dspec(lambda i: (i, 0, 0)), dspec(lambda i: (jnp.minimum(i + 1, last), 0, 0)),
                  _row_spec(grp, D_MODEL), _row_spec(grp, LANES), _mod_spec(grp, 5),
                  _const_spec(lg.shape), _const_spec(lb.shape), pl.BlockSpec(memory_space=pl.ANY)],
        out_specs=_row_spec(grp, D_MODEL),
        out_shape=jax.ShapeDtypeStruct((grp.rows, D_MODEL), F32),
        scratch_shapes=[pltpu.VMEM((2, TOP_K, tm * LANE_TILES, LANES), F32), pltpu.SemaphoreType.DMA((2, 2))],
        compiler_params=_params(1),
        name="combine_ln",
    )(dblk, dblk, x1, gates, mod, lg, lb, y_buf)


def kernel(x_prompt, x_sample, state_wkv, state_shift, state_conv, c_prompt, c_sample, w_in, tm_mu, w0, w2, a0, a2, g2, v0, v1, v2, k_k, k_a, r_k, lnx_g, lnx_b, conv_w, conv_b, conv_ln_g, conv_ln_b, w_out, ada_w, ada_b, ln1_g, ln1_b, ln2_g, ln2_b, router_w, router_b, w_gu, b_gu, w_dn, b_dn):
    bp, tp, _ = x_prompt.shape
    bs, ts, _ = x_sample.shape
    depth = w_in.shape[0]
    n_p, n_s = bp * tp, bs * ts
    n_tok = n_p + n_s
    gp = _Group(n_p, tp, 1, False)
    gs = _Group(n_s, n_s, bs, True)

    mod_all = _ada(jnp.concatenate([c_prompt, c_sample], axis=0), ada_w, ada_b)

    w_in_b = w_in.astype(BF16)
    w_out_b = w_out.astype(BF16)
    hid = jnp.arange(RWKV_DIM) // HEAD_DIM
    gmat = (hid[:, None] == hid[None, :]).astype(BF16)
    zero_wa = jnp.zeros((depth, DECAY_LORA, RWKV_DIM), F32)
    wwa = jnp.concatenate([jnp.concatenate([w2, zero_wa], axis=2),
                           jnp.concatenate([zero_wa, a2], axis=2)], axis=1).astype(BF16)
    v1p = jnp.pad(v1, ((0, 0), (0, 0), (0, LANES - VRES_LORA))).astype(BF16)
    v2p = jnp.pad(v2, ((0, 0), (0, LANES - VRES_LORA), (0, 0))).astype(BF16)
    rw_p = jnp.pad(router_w, ((0, 0), (0, 0), (0, LANES - N_EXPERTS)))
    rw_hi = rw_p.astype(BF16)
    rw_lo = (rw_p - rw_hi.astype(F32)).astype(BF16)
    rb_p = jnp.pad(router_b, ((0, 0), (0, LANES - N_EXPERTS)), constant_values=-1e30)
    b_gu4 = b_gu.reshape(depth, N_EXPERTS, 1, 2 * D_FF)
    b_dn4 = b_dn.reshape(depth, N_EXPERTS, 1, D_MODEL)

    def row2(a):
        return a.reshape(1, -1)

    x_p = x_prompt.reshape(n_p, D_MODEL)
    x_s = jnp.transpose(x_sample, (1, 0, 2)).reshape(n_s, D_MODEL)
    zeros_wkv = jnp.zeros((bp, RWKV_HEADS, HEAD_DIM, HEAD_DIM), F32)
    zeros_shift = jnp.zeros((bp, 1, SHIFT_DIM), F32)
    zeros_conv = jnp.zeros((bp, CONV_HIST, CONV_DIM), F32)
    wkv_s_t = jnp.transpose(state_wkv, (0, 2, 4, 3, 1))

    vf_p = vf_s = None
    wkv_p_out, wkv_s_out, shift_p_out, shift_s_out, conv_p_out, conv_s_out = [], [], [], [], [], []
    for l in range(depth):
        mod_p = mod_all[l, :bp].reshape(bp, 1, 6 * D_MODEL)
        mod_s = jnp.tile(mod_all[l, bp:], (ts, 1))
        wts = {"mu": row2(tm_mu[l]), "w0": row2(w0[l]), "a0": row2(a0[l]), "k_k": row2(k_k[l]),
               "k_a": row2(k_a[l]), "wwa": wwa[l], "g2": g2[l].astype(BF16),
               "lnx_g": row2(lnx_g[l]), "lnx_b": row2(lnx_b[l]), "r_k": row2(r_k[l]), "w_out": w_out_b[l],
               "ln1_g": row2(ln1_g[l]), "ln1_b": row2(ln1_b[l]), "router_w": rw_hi[l], "router_w_lo": rw_lo[l],
               "router_b": row2(rb_p[l])}
        if l > 0:
            wts.update({"v0": row2(v0[l - 1]), "v1": v1p[l - 1], "v2": v2p[l - 1]})

        per_group = []
        for grp, x, mod, is_p in ((gp, x_p, mod_p, True), (gs, x_s, mod_s, False)):
            if is_p:
                shift0, conv0 = zeros_shift, zeros_conv
            else:
                shift0 = state_shift[l][None]
                conv0 = jnp.transpose(state_conv[l], (1, 0, 2)).reshape(1, CONV_HIST * bs, CONV_DIM)
            vfirst = vf_p if is_p else vf_s
            r, lw, k, v, an, bn, g, zc, sh_last = _prep(grp, x, mod, w_in_b[l], shift0,
                                                        vfirst if l > 0 else None, wts, gmat)
            if l == 0:
                if is_p:
                    vf_p = v
                else:
                    vf_s = v
            if is_p:
                y, s_new = _wkv_chunked(bp, tp, r, lw, k, v, an, bn, zeros_wkv)
                wkv_p_out.append(s_new)
                shift_p_out.append(sh_last[:, 0])
            else:
                tmaj = lambda a: jnp.transpose(a.reshape(ts, bs, RWKV_DIM), (0, 2, 1))
                y_t, s_new = _wkv_lanes(ts, tmaj(r), tmaj(lw), tmaj(k), tmaj(v), tmaj(an), tmaj(bn), wkv_s_t[l])
                y = jnp.transpose(y_t, (0, 2, 1)).reshape(n_s, RWKV_DIM)
                wkv_s_out.append(jnp.transpose(s_new, (3, 0, 2, 1)))
                shift_s_out.append(sh_last[0])
            cv, tail = _conv(grp, zc, conv0, conv_w[l], row2(conv_b[l]), row2(conv_ln_g[l]), row2(conv_ln_b[l]))
            if is_p:
                conv_p_out.append(tail)
            else:
                conv_s_out.append(jnp.transpose(tail.reshape(CONV_HIST, bs, CONV_DIM), (1, 0, 2)))
            off = 0 if is_p else n_p
            x1, h2, ti, gt = _mix(grp, x, y, r, k, v, g, cv, mod, wts, gmat)
            per_group.append((grp, x1, h2, ti, gt, mod, off))

        top_i = jnp.concatenate([pg[3] for pg in per_group], axis=0)
        dest, block_e, n_used, n_rows = _route(top_i, gp.tm)
        if l == 0:
            x_buf = jnp.zeros((n_rows * LANE_TILES, LANES), F32)
        for grp, x1, h2, ti, gt, mod, off in per_group:
            x_buf = _dispatch(grp, h2, dest[off:off + grp.rows], x_buf)
        y_buf = _experts(l, x_buf, block_e, n_used, w_gu, b_gu4, w_dn, b_dn4)
        outs = []
        for grp, x1, h2, ti, gt, mod, off in per_group:
            outs.append(_combine(grp, x1, gt, dest[off:off + grp.rows], y_buf, mod, row2(ln2_g[l]), row2(ln2_b[l])))
        x_p, x_s = outs

    y_prompt = x_p.reshape(bp, tp, D_MODEL)
    y_sample = jnp.transpose(x_s.reshape(ts, bs, D_MODEL), (1, 0, 2))
    return (y_prompt, y_sample, jnp.stack(wkv_p_out), jnp.stack(wkv_s_out), jnp.stack(shift_p_out),
            jnp.stack(shift_s_out), jnp.stack(conv_p_out), jnp.stack(conv_s_out))
```

```python
import functools

import jax
import jax.numpy as jnp
from jax import lax
from jax.experimental import pallas as pl
from jax.experimental.pallas import tpu as pltpu

F32 = jnp.float32
BF16 = jnp.bfloat16

D_MODEL = 1024
DEPTH = 4
HEAD_DIM = 64
RWKV_DIM = 512
RWKV_HEADS = RWKV_DIM // HEAD_DIM
CONV_DIM = 512
CONV_WIDTH = 31
CONV_HIST = CONV_WIDTH - 1
DECAY_LORA = 64
AAA_LORA = 64
GATE_LORA = 128
VRES_LORA = 32
SHIFT_DIM = 3 * RWKV_DIM + DECAY_LORA + AAA_LORA + GATE_LORA
PROJ_DIM = SHIFT_DIM + 2 * CONV_DIM
N_EXPERTS = 32
TOP_K = 4
D_FF = D_MODEL
SWIGLU_LIMIT = 7.0
SWIGLU_ALPHA = 1.702
LN_EPS = 1e-5
GN_EPS = 64e-5
DEEPNORM_ALPHA = (2 * DEPTH) ** 0.25

LANES = 128
SUBLANES = 8
ROW_TILE = 512
CHUNK = 64
WKV_BATCHES_PER_STEP = 2
EXPERT_ROWS = 512
VMEM_LIMIT = 56 * 1024 * 1024


def _params(n_axes=1, vmem=VMEM_LIMIT):
    return pltpu.CompilerParams(dimension_semantics=("arbitrary",) * n_axes, vmem_limit_bytes=vmem)


def _round_up(x, m):
    return (x + m - 1) // m * m


def _bdot(a, b):
    return jnp.dot(a.astype(BF16), b.astype(BF16), preferred_element_type=F32)


def _split3(x):
    hi = x.astype(BF16)
    r1 = x - hi.astype(F32)
    mid = r1.astype(BF16)
    lo = (r1 - mid.astype(F32)).astype(BF16)
    return hi, mid, lo


def _headsum(x, g):
    return jnp.dot(x.astype(BF16), g, preferred_element_type=F32)


LANE_TILES = D_MODEL // LANES
assert LANE_TILES == SUBLANES


def _store_row_tiles(ref, x, rows, lead=()):
    for c in range(LANE_TILES):
        ref[lead + (pl.ds(c, rows, stride=LANE_TILES), slice(None))] = x[:, c * LANES:(c + 1) * LANES]


def _load_row_tiles(ref, rows, lead=()):
    return [ref[lead + (pl.ds(c, rows, stride=LANE_TILES), slice(None))] for c in range(LANE_TILES)]


def _row_tile(ref, r, lead=()):
    return ref.at[lead + (pl.ds(pl.multiple_of(r * LANE_TILES, LANE_TILES), LANE_TILES), slice(None))]


def _layernorm(x):
    mu = jnp.mean(x, axis=-1, keepdims=True)
    xc = x - mu
    var = jnp.mean(xc * xc, axis=-1, keepdims=True)
    return xc * lax.rsqrt(var + LN_EPS)


def _ada_body(c_ref, w_ref, b_ref, o_ref):
    c = c_ref[...]
    cs = c * jax.nn.sigmoid(c)
    o_ref[...] = _bdot(cs, w_ref[...]) + b_ref[...]


def _ada(c_all, ada_w, ada_b):
    depth, _, width = ada_w.shape
    nb = c_all.shape[0]
    tn = 1536
    return pl.pallas_call(
        _ada_body,
        grid=(depth, width // tn),
        in_specs=[pl.BlockSpec((nb, D_MODEL), lambda l, j: (0, 0)),
                  pl.BlockSpec((None, D_MODEL, tn), lambda l, j: (l, 0, j)),
                  pl.BlockSpec((None, 1, tn), lambda l, j: (l, 0, j))],
        out_specs=pl.BlockSpec((None, nb, tn), lambda l, j: (l, 0, j)),
        out_shape=jax.ShapeDtypeStruct((depth, nb, width), F32),
        compiler_params=_params(2),
        name="ada_mod",
    )(c_all, ada_w, ada_b.reshape(depth, 1, width))


class _Group:
    def __init__(self, rows, seg_len, stride, mod_per_row):
        self.rows = rows
        self.seg_len = seg_len
        self.stride = stride
        self.mod_per_row = mod_per_row
        self.tm = min(ROW_TILE, rows)
        assert rows % self.tm == 0 and seg_len % self.tm == 0
        self.tiles = rows // self.tm
        self.tiles_per_seg = seg_len // self.tm
        self.nseg = rows // seg_len


def _mod_spec(grp, which):
    if grp.mod_per_row:
        return pl.BlockSpec((grp.tm, D_MODEL), lambda i: (i, which))
    tps = grp.tiles_per_seg
    return pl.BlockSpec((None, 1, D_MODEL), lambda i: (i // tps, 0, which))


def _row_spec(grp, width):
    return pl.BlockSpec((grp.tm, width), lambda i: (i, 0))


def _const_spec(shape):
    nd = len(shape)
    return pl.BlockSpec(shape, lambda i: (0,) * nd)


def _prep_body(has_vres, grp, halo, *refs):
    it = iter(refs)
    x_ref, sc_ref, sh_ref, win_ref, sh0_ref = (next(it) for _ in range(5))
    vf_ref = next(it) if has_vres else None
    mu_ref, w0_ref, a0_ref, kk_ref, ka_ref, wwa_ref, g2_ref = (next(it) for _ in range(7))
    if has_vres:
        v0_ref, v1_ref, v2_ref = next(it), next(it), next(it)
    gmat_ref = next(it)
    r_o, lw_o, k_o, v_o, an_o, bn_o, g_o, shl_o, zc_o = (next(it) for _ in range(9))
    ext = next(it)

    tm, s = grp.tm, grp.stride
    j = pl.program_id(0) % grp.tiles_per_seg

    @pl.when(j == 0)
    def _():
        ext[pl.ds(halo - s, s), :] = sh0_ref[...]

    h = x_ref[...] * (1.0 + sc_ref[...]) + sh_ref[...]
    z = jnp.dot(h.astype(BF16), win_ref[...], preferred_element_type=F32)
    zc_o[...] = z[:, SHIFT_DIM:]
    zs = z[:, :SHIFT_DIM]
    ext[pl.ds(halo, tm), :] = zs
    prev = ext[pl.ds(halo - s, tm), :]
    shl_o[...] = ext[pl.ds(halo + tm - s, s), :]
    if grp.tiles_per_seg > 1:
        ext[pl.ds(0, halo), :] = ext[pl.ds(tm, halo), :]

    zm = zs + (prev - zs) * mu_ref[...]
    i1, i2, i3 = RWKV_DIM, 2 * RWKV_DIM, 3 * RWKV_DIM
    i5 = i3 + DECAY_LORA + AAA_LORA
    r, k, v = zm[:, :i1], zm[:, i1:i2], zm[:, i2:i3]
    zwa, zg = zm[:, i3:i5], zm[:, i5:]
    lane = lax.broadcasted_iota(jnp.int32, zwa.shape, 1)
    xwa = jnp.where(lane < DECAY_LORA, jnp.tanh(zwa), zwa)
    wa = _bdot(xwa, wwa_ref[...])
    w_in = w0_ref[...] + wa[:, :RWKV_DIM]
    neg = -w_in
    softplus = jnp.maximum(neg, 0.0) + jnp.log(1.0 + jnp.exp(-jnp.abs(neg)))
    w_log = -softplus - 0.5
    lw = -jnp.exp(w_log)
    a = jax.nn.sigmoid(a0_ref[...] + wa[:, RWKV_DIM:])
    g = _bdot(jax.nn.sigmoid(zg), g2_ref[...])
    if has_vres:
        vl = _bdot(_bdot(v, v1_ref[...]), v2_ref[...])
        vg = jax.nn.sigmoid(v0_ref[...] + vl)
        v = v + (vf_ref[...] - v) * vg
    kk = k * kk_ref[...]
    ss = _headsum(kk * kk, gmat_ref[...])
    kk = kk * lax.rsqrt(jnp.maximum(ss, 1e-24))
    k = k * (1.0 + (a - 1.0) * ka_ref[...])
    r_o[...] = r
    lw_o[...] = lw
    k_o[...] = k
    v_o[...] = v
    an_o[...] = -kk
    bn_o[...] = kk * a
    g_o[...] = g


def _prep(grp, x, mod, w_in_b, shift0, vfirst, wts, gmat):
    has_vres = vfirst is not None
    halo = _round_up(grp.stride, SUBLANES)
    s = grp.stride
    tps = grp.tiles_per_seg
    seg_spec = pl.BlockSpec((None, s, SHIFT_DIM), lambda i: (i // tps, 0, 0))
    in_specs = [_row_spec(grp, D_MODEL), _mod_spec(grp, 1), _mod_spec(grp, 0),
                _const_spec((D_MODEL, PROJ_DIM)), seg_spec]
    args = [x, mod, mod, w_in_b, shift0]
    if has_vres:
        in_specs.append(_row_spec(grp, RWKV_DIM))
        args.append(vfirst)
    names = ["mu", "w0", "a0", "k_k", "k_a", "wwa", "g2"] + (["v0", "v1", "v2"] if has_vres else [])
    for nm in names:
        in_specs.append(_const_spec(wts[nm].shape))
        args.append(wts[nm])
    in_specs.append(_const_spec(gmat.shape))
    args.append(gmat)
    out = jax.ShapeDtypeStruct((grp.rows, RWKV_DIM), F32)
    return pl.pallas_call(
        functools.partial(_prep_body, has_vres, grp, halo),
        grid=(grp.tiles,),
        in_specs=in_specs,
        out_specs=[_row_spec(grp, RWKV_DIM)] * 7 + [seg_spec, _row_spec(grp, 2 * CONV_DIM)],
        out_shape=[out] * 7 + [jax.ShapeDtypeStruct((grp.nseg, s, SHIFT_DIM), F32),
                               jax.ShapeDtypeStruct((grp.rows, 2 * CONV_DIM), F32)],
        scratch_shapes=[pltpu.VMEM((halo + grp.tm, SHIFT_DIM), F32)],
        compiler_params=_params(1),
        name="inproj_prep",
    )(*args)


def _nt(a, b):
    return lax.dot_general(a.astype(BF16), b.astype(BF16), (((1,), (1,)), ((), ())),
                           preferred_element_type=F32)


def _tn(a, b):
    return lax.dot_general(a.astype(BF16), b.astype(BF16), (((0,), (0,)), ((), ())),
                           preferred_element_type=F32)


def _wkv_chunk_body(nchunks, nb, r_ref, lw_ref, k_ref, v_ref, a_ref, b_ref, s0_ref, y_ref, so_ref, st):
    c = pl.program_id(1)
    L, N = CHUNK, HEAD_DIM

    @pl.when(c == 0)
    def _():
        st[...] = s0_ref[...]

    P2, W2 = 2 * L, 2 * N
    assert L == N and W2 == LANES
    trow = lax.broadcasted_iota(jnp.int32, (L, L), 0)
    tcol = lax.broadcasted_iota(jnp.int32, (L, L), 1)
    tri = jnp.where(trow >= tcol, 1.0, 0.0).astype(BF16)
    row = lax.broadcasted_iota(jnp.int32, (P2, P2), 0)
    col = lax.broadcasted_iota(jnp.int32, (P2, P2), 1)
    lg = L.bit_length() - 1
    same_head = (row >> lg) == (col >> lg)
    strict = same_head & (row > col)
    incl = same_head & (row >= col)
    eye = row == col
    sh = 3
    diag_blk = (row >> sh) == (col >> sh)
    off_masks = []
    while (1 << sh) < L:
        off_masks.append(((row >> sh) == (col >> sh) + 1) & ((row >> (sh + 1)) == (col >> (sh + 1))))
        sh += 1

    lane_lo = lax.broadcasted_iota(jnp.int32, (L, W2), 1) < N
    npair = RWKV_HEADS // 2

    def stack(x, p):
        xp = x[:, p * W2:(p + 1) * W2]
        return jnp.concatenate([jnp.where(lane_lo, xp, 0.0), jnp.where(lane_lo, 0.0, xp)], axis=0)

    at, rt, bt, kt, bh, kh, vv, p_last = [], [], [], [], [], [], [], []
    for bi in range(nb):
        lw = lw_ref[bi]
        h3 = _split3(lw)
        cl = (jnp.dot(tri, h3[0], preferred_element_type=F32) + jnp.dot(tri, h3[1], preferred_element_type=F32)
              + jnp.dot(tri, h3[2], preferred_element_type=F32))
        cl_last = cl[L - 1:L, :]
        ecl = jnp.exp(cl)
        encl = jnp.exp(-cl)
        dl = jnp.exp(cl_last - cl)
        pl_row = jnp.exp(cl_last)
        an, bn, kx, rx, vx = a_ref[bi], b_ref[bi], k_ref[bi], r_ref[bi], v_ref[bi]
        at_all, rt_all = an * jnp.exp(cl - lw), rx * ecl
        bt_all, kt_all, bh_all, kh_all = bn * encl, kx * encl, bn * dl, kx * dl
        for p in range(npair):
            at.append(stack(at_all, p).astype(BF16))
            rt.append(stack(rt_all, p))
            bt.append(stack(bt_all, p).astype(BF16))
            kt.append(stack(kt_all, p).astype(BF16))
            bh.append(stack(bh_all, p).astype(BF16))
            kh.append(stack(kh_all, p).astype(BF16))
            vv.append(stack(vx, p).astype(BF16))
            p_last.append(pl_row[:, p * W2:(p + 1) * W2])
    pairs = range(nb * npair)

    ar = [jnp.concatenate([at[p], rt[p].astype(BF16)], axis=0) for p in pairs]
    xb = [_nt(ar[p], bt[p]) for p in pairs]
    xk = [_nt(ar[p], kt[p]) for p in pairs]
    a_ab = [jnp.where(strict, xb[p][:P2], 0.0) for p in pairs]
    a_ak = [jnp.where(strict, xk[p][:P2], 0.0) for p in pairs]
    a_rb = [jnp.where(incl, xb[p][P2:], 0.0).astype(BF16) for p in pairs]
    a_rk = [jnp.where(incl, xk[p][P2:], 0.0) for p in pairs]
    pw = [jnp.where(diag_blk, a_ab[p], 0.0) for p in pairs]
    tm = [jnp.where(eye, 1.0, 0.0) + pw[p] for p in pairs]
    for _ in range(2):
        pw = [_bdot(pw[p], pw[p]) for p in pairs]
        tm = [tm[p] + _bdot(tm[p], pw[p]) for p in pairs]
    for off in off_masks:
        t1 = [_bdot(jnp.where(off, a_ab[p], 0.0), tm[p]) for p in pairs]
        tm = [tm[p] + _bdot(tm[p], t1[p]) for p in pairs]
    akv = [_bdot(a_ak[p], vv[p]).astype(BF16) for p in pairs]
    wu = [_bdot(tm[p], jnp.concatenate([at[p], akv[p]], axis=1)) for p in pairs]
    qy = [jnp.dot(a_rb[p], wu[p].astype(BF16), preferred_element_type=F32) for p in pairs]
    rkv = [_bdot(a_rk[p], vv[p]) for p in pairs]
    ys = []
    for p in pairs:
        bi, pp = divmod(p, npair)
        s_p = st[bi, pp]
        w, u0 = wu[p][:, :W2], wu[p][:, W2:]
        q = rt[p] + qy[p][:, :W2]
        ybd = _nt(q, s_p) + qy[p][:, W2:] + rkv[p]
        ys.append(ybd[:L] + ybd[L:])
        cm = _tn(u0, bh[p]) + _tn(vv[p], kh[p])
        st[bi, pp] = s_p * p_last[p] + _bdot(s_p, _tn(w, bh[p])) + cm
    for bi in range(nb):
        y_ref[bi] = jnp.concatenate(ys[bi * npair:(bi + 1) * npair], axis=1)

    @pl.when(c == nchunks - 1)
    def _():
        so_ref[...] = st[...]


def _wkv_chunked(nbatch, tlen, r, lw, k, v, an, bn, s0):
    nchunks = tlen // CHUNK
    npair, w2 = RWKV_HEADS // 2, 2 * HEAD_DIM
    s5 = s0.reshape(nbatch, npair, 2, HEAD_DIM, HEAD_DIM)
    zero = jnp.zeros_like(s5[:, :, 0])
    s_bd = jnp.concatenate([jnp.concatenate([s5[:, :, 0], zero], axis=-1),
                            jnp.concatenate([zero, s5[:, :, 1]], axis=-1)], axis=-2)
    nb = WKV_BATCHES_PER_STEP if nbatch % WKV_BATCHES_PER_STEP == 0 else 1
    tok = pl.BlockSpec((nb, CHUNK, RWKV_DIM), lambda b, c: (b, c, 0))
    stt = pl.BlockSpec((nb, npair, w2, w2), lambda b, c: (b, 0, 0, 0))
    seq = lambda t: t.reshape(nbatch, tlen, RWKV_DIM)
    y, so = pl.pallas_call(
        functools.partial(_wkv_chunk_body, nchunks, nb),
        grid=(nbatch // nb, nchunks),
        in_specs=[tok] * 6 + [stt],
        out_specs=[tok, stt],
        out_shape=[jax.ShapeDtypeStruct((nbatch, tlen, RWKV_DIM), F32),
                   jax.ShapeDtypeStruct((nbatch, npair, w2, w2), F32)],
        scratch_shapes=[pltpu.VMEM((nb, npair, w2, w2), F32)],
        compiler_params=_params(2),
        name="wkv_chunked",
    )(seq(r), seq(lw), seq(k), seq(v), seq(an), seq(bn), s_bd)
    s_new = jnp.stack([so[:, :, :HEAD_DIM, :HEAD_DIM], so[:, :, HEAD_DIM:, HEAD_DIM:]], axis=2)
    return y.reshape(nbatch * tlen, RWKV_DIM), s_new.reshape(nbatch, RWKV_HEADS, HEAD_DIM, HEAD_DIM)


def _wkv_lane_body(tlen, r_ref, lw_ref, k_ref, v_ref, a_ref, b_ref, s_ref, y_ref, so_ref):
    N = HEAD_DIM
    nb = s_ref.shape[-1]

    def row(ref, t, j):
        return jnp.broadcast_to(ref[t, pl.ds(j, 1), :], (N, nb))

    def first(j, acc):
        return acc + s_ref[j] * row(a_ref, 0, j)

    sa = lax.fori_loop(0, N, first, jnp.zeros((N, nb), F32))
    for t in range(tlen):
        v_t = v_ref[t]
        src = s_ref if t == 0 else so_ref

        def body(j, carry, t=t, v_t=v_t, src=src, sa=sa):
            y, sa_next = carry
            sj = src[j] * jnp.exp(row(lw_ref, t, j)) + sa * row(b_ref, t, j) + v_t * row(k_ref, t, j)
            so_ref[j] = sj
            y = y + sj * row(r_ref, t, j)
            if t + 1 < tlen:
                sa_next = sa_next + sj * row(a_ref, t + 1, j)
            return y, sa_next

        zero = jnp.zeros((N, nb), F32)
        y, sa = lax.fori_loop(0, N, body, (zero, zero))
        y_ref[t] = y


def _wkv_lanes(tlen, r, lw, k, v, an, bn, s_t):
    nb = r.shape[-1]
    tok = pl.BlockSpec((tlen, HEAD_DIM, nb), lambda h: (0, h, 0))
    stt = pl.BlockSpec((None, HEAD_DIM, HEAD_DIM, nb), lambda h: (h, 0, 0, 0))
    return pl.pallas_call(
        functools.partial(_wkv_lane_body, tlen),
        grid=(RWKV_HEADS,),
        in_specs=[tok] * 6 + [stt],
        out_specs=[tok, stt],
        out_shape=[jax.ShapeDtypeStruct((tlen, RWKV_DIM, nb), F32),
                   jax.ShapeDtypeStruct((RWKV_HEADS, HEAD_DIM, HEAD_DIM, nb), F32)],
        compiler_params=_params(1),
        name="wkv_lanes",
    )(r, lw, k, v, an, bn, s_t)


def _conv_body(grp, halo, zc_ref, c0_ref, cw_ref, cb_ref, lg_ref, lb_ref, cv_ref, tail_ref, ext, win):
    tm, s = grp.tm, grp.stride
    hist = CONV_HIST * s
    j = pl.program_id(0) % grp.tiles_per_seg

    @pl.when(j == 0)
    def _():
        ext[pl.ds(halo - hist, hist), :] = c0_ref[...]

    zc = zc_ref[...]
    u = zc[:, :CONV_DIM] * jax.nn.sigmoid(zc[:, CONV_DIM:])
    ext[pl.ds(halo, tm), :] = u
    acc = jnp.broadcast_to(cb_ref[...], (tm, CONV_DIM))
    offs = [halo - hist + w * s for w in range(CONV_WIDTH)]
    for phase in range(SUBLANES):
        taps = [w for w in range(CONV_WIDTH) if offs[w] % SUBLANES == phase]
        if not taps:
            continue
        if s % SUBLANES == 0:
            for w in taps:
                acc = acc + cw_ref[pl.ds(w, 1), :] * ext[pl.ds(offs[w], tm), :]
            continue
        lo = offs[taps[0]]
        span = tm + offs[taps[-1]] - lo
        win[pl.ds(0, span), :] = ext[pl.ds(lo, span), :]
        for w in taps:
            acc = acc + cw_ref[pl.ds(w, 1), :] * win[pl.ds(offs[w] - lo, tm), :]
    y = _layernorm(acc) * lg_ref[...] + lb_ref[...]
    cv_ref[...] = y * jax.nn.sigmoid(y)
    tail_ref[...] = ext[pl.ds(halo + tm - hist, hist), :]
    if grp.tiles_per_seg > 1:
        ext[pl.ds(0, halo), :] = ext[pl.ds(tm, halo), :]


def _conv(grp, zc, conv0, cw, cb, lg, lb):
    hist = CONV_HIST * grp.stride
    halo = _round_up(hist, SUBLANES)
    tps = grp.tiles_per_seg
    hist_spec = pl.BlockSpec((None, hist, CONV_DIM), lambda i: (i // tps, 0, 0))
    return pl.pallas_call(
        functools.partial(_conv_body, grp, halo),
        grid=(grp.tiles,),
        in_specs=[_row_spec(grp, 2 * CONV_DIM), hist_spec,
                  _const_spec(cw.shape), _const_spec(cb.shape), _const_spec(lg.shape), _const_spec(lb.shape)],
        out_specs=[_row_spec(grp, CONV_DIM), hist_spec],
        out_shape=[jax.ShapeDtypeStruct((grp.rows, CONV_DIM), F32),
                   jax.ShapeDtypeStruct((grp.nseg, hist, CONV_DIM), F32)],
        scratch_shapes=[pltpu.VMEM((halo + grp.tm, CONV_DIM), F32),
                        pltpu.VMEM((grp.tm + _round_up(CONV_HIST, SUBLANES), CONV_DIM), F32)],
        compiler_params=_params(1),
        name="conv_group",
    )(zc, conv0, cw, cb, lg, lb)


def _mix_body(x_ref, y_ref, r_ref, k_ref, v_ref, g_ref, cv_ref, g1_ref, sc2_ref, sh2_ref,
              lnxg_ref, lnxb_ref, rk_ref, wout_ref, l1g_ref, l1b_ref, rw_ref, rwl_ref, rb_ref, gmat_ref,
              x1_ref, h2_ref, ti_ref, gt_ref):
    gm = gmat_ref[...]
    y = y_ref[...]
    inv = 1.0 / HEAD_DIM
    ym = _headsum(y, gm) * inv
    yc = y - ym
    yv = _headsum(yc * yc, gm) * inv
    yn = yc * lax.rsqrt(yv + GN_EPS) * lnxg_ref[...] + lnxb_ref[...]
    bonus = _headsum(r_ref[...] * k_ref[...] * rk_ref[...], gm) * v_ref[...]
    yo = (yn + bonus) * g_ref[...]
    mix = (jnp.dot(yo.astype(BF16), wout_ref[pl.ds(0, RWKV_DIM), :], preferred_element_type=F32)
           + jnp.dot(cv_ref[...].astype(BF16), wout_ref[pl.ds(RWKV_DIM, CONV_DIM), :],
                     preferred_element_type=F32))
    xa = DEEPNORM_ALPHA * x_ref[...] + g1_ref[...] * mix
    x1 = _layernorm(xa) * l1g_ref[...] + l1b_ref[...]
    x1_ref[...] = x1
    h2 = x1 * (1.0 + sc2_ref[...]) + sh2_ref[...]
    _store_row_tiles(h2_ref, h2, h2.shape[0])
    h_hi = h2.astype(BF16)
    h_lo = (h2 - h_hi.astype(F32)).astype(BF16)
    logits = (jnp.dot(h_hi, rw_ref[...], preferred_element_type=F32)
              + jnp.dot(h_hi, rwl_ref[...], preferred_element_type=F32)
              + jnp.dot(h_lo, rw_ref[...], preferred_element_type=F32)) + rb_ref[...]
    lane = lax.broadcasted_iota(jnp.int32, logits.shape, 1).astype(F32)
    cur = logits
    vals, idxs = [], []
    for _ in range(TOP_K):
        m = jnp.max(cur, axis=-1, keepdims=True)
        idx = jnp.min(jnp.where(cur == m, lane, float(LANES)), axis=-1, keepdims=True)
        vals.append(m)
        idxs.append(idx)
        cur = jnp.where(lane == idx, -jnp.inf, cur)
    es = [jnp.exp(vv - vals[0]) for vv in vals]
    den = es[0] + es[1] + es[2] + es[3]
    ti = jnp.zeros_like(logits)
    gt = jnp.zeros_like(logits)
    for kk in range(TOP_K):
        ti = jnp.where(lane == float(kk), idxs[kk], ti)
        gt = jnp.where(lane == float(kk), es[kk] / den, gt)
    ti_ref[...] = ti.astype(jnp.int32)
    gt_ref[...] = gt


def _mix(grp, x, y, r, k, v, g, cv, mod, wts, gmat):
    half = _row_spec(grp, RWKV_DIM)
    names = ["lnx_g", "lnx_b", "r_k", "w_out", "ln1_g", "ln1_b", "router_w", "router_w_lo", "router_b"]
    in_specs = ([_row_spec(grp, D_MODEL)] + [half] * 6 + [_mod_spec(grp, 2), _mod_spec(grp, 4), _mod_spec(grp, 3)]
                + [_const_spec(wts[nm].shape) for nm in names] + [_const_spec(gmat.shape)])
    args = [x, y, r, k, v, g, cv, mod, mod, mod] + [wts[nm] for nm in names] + [gmat]
    return pl.pallas_call(
        _mix_body,
        grid=(grp.tiles,),
        in_specs=in_specs,
        out_specs=[_row_spec(grp, D_MODEL), pl.BlockSpec((grp.tm * LANE_TILES, LANES), lambda i: (i, 0)),
                   _row_spec(grp, LANES), _row_spec(grp, LANES)],
        out_shape=[jax.ShapeDtypeStruct((grp.rows, D_MODEL), F32),
                   jax.ShapeDtypeStruct((grp.rows * LANE_TILES, LANES), F32),
                   jax.ShapeDtypeStruct((grp.rows, LANES), jnp.int32),
                   jax.ShapeDtypeStruct((grp.rows, LANES), F32)],
        compiler_params=_params(1),
        name="mix_out",
    )(*args)


def _route_body(ti_ref, dest_ref, cnt_ref, carry, pstart):
    phase = pl.program_id(0)
    i = pl.program_id(1)
    ti = ti_ref[...]
    tm = ti.shape[0]
    lane = lax.broadcasted_iota(jnp.int32, (tm, LANES), 1)
    sel = [lane == ti[:, kk:kk + 1] for kk in range(TOP_K)]
    onehot = jnp.zeros((tm, LANES), F32)
    for kk in range(TOP_K):
        onehot = onehot + jnp.where(sel[kk], 1.0, 0.0)

    @pl.when((phase == 0) & (i == 0))
    def _():
        carry[...] = jnp.zeros(carry.shape, F32)

    @pl.when((phase == 1) & (i == 0))
    def _():
        cnt = carry[...]
        cnt_ref[...] = cnt
        padded = jnp.floor((cnt + (EXPERT_ROWS - 1)) * (1.0 / EXPERT_ROWS)) * EXPERT_ROWS
        er = lax.broadcasted_iota(jnp.int32, (LANES, LANES), 0)
        ec = lax.broadcasted_iota(jnp.int32, (LANES, LANES), 1)
        earlier = jnp.where(er < ec, 1.0, 0.0).astype(BF16)
        h3 = _split3(jnp.broadcast_to(padded, (SUBLANES, LANES)))
        acc = (jnp.dot(h3[0], earlier, preferred_element_type=F32)
               + jnp.dot(h3[1], earlier, preferred_element_type=F32)
               + jnp.dot(h3[2], earlier, preferred_element_type=F32))
        pstart[...] = acc[0:1, :]
        carry[...] = jnp.zeros(carry.shape, F32)

    @pl.when(phase == 1)
    def _():
        rr = lax.broadcasted_iota(jnp.int32, (tm, tm), 0)
        cc = lax.broadcasted_iota(jnp.int32, (tm, tm), 1)
        before = jnp.where(rr > cc, 1.0, 0.0).astype(BF16)
        cum = jnp.dot(before, onehot.astype(BF16), preferred_element_type=F32) + (carry[...] + pstart[...])
        dest = jnp.zeros((tm, LANES), F32)
        for kk in range(TOP_K):
            pk = jnp.sum(jnp.where(sel[kk], cum, 0.0), axis=-1, keepdims=True)
            dest = jnp.where(lane == kk, pk, dest)
        dest_ref[...] = dest.astype(jnp.int32)

    carry[...] = carry[...] + jnp.sum(onehot, axis=0, keepdims=True)


def _route(top_i, tm):
    n_tok = top_i.shape[0]
    rows = EXPERT_ROWS
    dest, cnt = pl.pallas_call(
        _route_body,
        grid=(2, n_tok // tm),
        in_specs=[pl.BlockSpec((tm, LANES), lambda ph, i: (i, 0))],
        out_specs=[pl.BlockSpec((tm, LANES), lambda ph, i: (i * ph, 0)),
                   pl.BlockSpec((1, LANES), lambda ph, i: (0, 0))],
        out_shape=[jax.ShapeDtypeStruct((n_tok, LANES), jnp.int32), jax.ShapeDtypeStruct((1, LANES), F32)],
        scratch_shapes=[pltpu.VMEM((1, LANES), F32), pltpu.VMEM((1, LANES), F32)],
        compiler_params=_params(2),
        name="route_rank",
    )(top_i)
    n_blocks = -(-(n_tok * TOP_K) // rows) + N_EXPERTS
    counts = cnt[0, :N_EXPERTS].astype(jnp.int32)
    padded_end = jnp.cumsum((counts + rows - 1) // rows * rows)
    block_start = jnp.arange(n_blocks, dtype=jnp.int32) * rows
    block_e = jnp.sum((padded_end[None, :] <= block_start[:, None]).astype(jnp.int32), axis=1)
    block_e = jnp.minimum(block_e, N_EXPERTS - 1)
    n_used = (padded_end[-1] // rows).astype(jnp.int32).reshape(1)
    return dest[:, :TOP_K], block_e, n_used, n_blocks * rows


def _dispatch_body(tm, dst_ref, h_ref, xin_ref, x_hbm, sem):
    del xin_ref

    def body(r, carry):
        for kk in range(TOP_K):
            d = dst_ref[0, r * TOP_K + kk]
            pltpu.make_async_copy(_row_tile(h_ref, r), _row_tile(x_hbm, d), sem.at[kk % 2]).start(priority=kk % 2)
        return carry

    lax.fori_loop(0, tm, body, 0, unroll=8)
    for kk in range(TOP_K):
        pltpu.make_async_copy(h_ref, h_ref, sem.at[kk % 2]).wait()


def _dispatch(grp, h2, dest, x_buf):
    tm = grp.tm
    dblk = dest.reshape(grp.tiles, 1, tm * TOP_K)
    return pl.pallas_call(
        functools.partial(_dispatch_body, tm),
        grid=(grp.tiles,),
        in_specs=[pl.BlockSpec((None, 1, tm * TOP_K), lambda i: (i, 0, 0), memory_space=pltpu.SMEM),
                  pl.BlockSpec((tm * LANE_TILES, LANES), lambda i: (i, 0)), pl.BlockSpec(memory_space=pl.ANY)],
        out_specs=pl.BlockSpec(memory_space=pl.ANY),
        out_shape=jax.ShapeDtypeStruct(x_buf.shape, F32),
        scratch_shapes=[pltpu.SemaphoreType.DMA((2,))],
        input_output_aliases={2: 0},
        compiler_params=pltpu.CompilerParams(dimension_semantics=("arbitrary",), vmem_limit_bytes=VMEM_LIMIT,
                                             has_side_effects=True),
        name="dispatch",
    )(dblk, h2, x_buf)


def _expert_body(be_ref, nu_ref, x_ref, wgu_ref, bgu_ref, wdn_ref, bdn_ref, o_ref, wgu_b, wdn_b):
    i = pl.program_id(0)

    @pl.when(i < nu_ref[0])
    def _():
        prev_e = be_ref[jnp.maximum(i - 1, 0)]

        @pl.when((i == 0) | (be_ref[i] != prev_e))
        def _():
            wgu_b[...] = wgu_ref[...].astype(BF16)
            wdn_b[...] = wdn_ref[...].astype(BF16)

        x = jnp.concatenate(_load_row_tiles(x_ref, EXPERT_ROWS), axis=1).astype(BF16)
        gu = jnp.dot(x, wgu_b[...], preferred_element_type=F32) + bgu_ref[...]
        gate = jnp.minimum(gu[:, :D_FF], SWIGLU_LIMIT)
        lin = jnp.clip(gu[:, D_FF:], -SWIGLU_LIMIT, SWIGLU_LIMIT)
        act = gate * jax.nn.sigmoid(SWIGLU_ALPHA * gate) * (lin + 1.0)
        out = jnp.dot(act.astype(BF16), wdn_b[...], preferred_element_type=F32) + bdn_ref[...]
        _store_row_tiles(o_ref, out, EXPERT_ROWS)

    @pl.when(i >= nu_ref[0])
    def _():
        o_ref[...] = jnp.zeros(o_ref.shape, F32)


def _experts(layer, x_buf, block_e, n_used, w_gu, b_gu, w_dn, b_dn):
    rows = EXPERT_ROWS
    n_blocks = x_buf.shape[0] // (rows * LANE_TILES)
    xblk = (rows * LANE_TILES, LANES)

    def blk(i, be, nu):
        return jnp.minimum(i, nu[0] - 1)

    def wmap(i, be, nu):
        return (layer, be[blk(i, be, nu)], 0, 0)

    grid_spec = pltpu.PrefetchScalarGridSpec(
        num_scalar_prefetch=2,
        grid=(n_blocks,),
        in_specs=[pl.BlockSpec(xblk, lambda i, be, nu: (blk(i, be, nu), 0)),
                  pl.BlockSpec((None, None, D_MODEL, 2 * D_FF), wmap),
                  pl.BlockSpec((None, None, 1, 2 * D_FF), wmap),
                  pl.BlockSpec((None, None, D_FF, D_MODEL), wmap),
                  pl.BlockSpec((None, None, 1, D_MODEL), wmap)],
        out_specs=pl.BlockSpec(xblk, lambda i, be, nu: (i, 0)),
        scratch_shapes=[pltpu.VMEM((D_MODEL, 2 * D_FF), BF16), pltpu.VMEM((D_FF, D_MODEL), BF16)])
    return pl.pallas_call(
        _expert_body,
        grid_spec=grid_spec,
        out_shape=jax.ShapeDtypeStruct(x_buf.shape, F32),
        compiler_params=_params(1),
        name="experts",
    )(block_e, n_used, x_buf, w_gu, b_gu, w_dn, b_dn)


def _combine_body(tm, dcur_ref, dnxt_ref, x1_ref, gt_ref, g2_ref, lg_ref, lb_ref, y_hbm, o_ref, ybuf, sem):
    i = pl.program_id(0)
    slot = i % 2

    def gather(dref, sl):
        def body(r, carry):
            for kk in range(TOP_K):
                d = dref[0, r * TOP_K + kk]
                pltpu.make_async_copy(_row_tile(y_hbm, d), _row_tile(ybuf, r, (sl, kk)),
                                      sem.at[sl, kk % 2]).start(priority=kk % 2)
            return carry
        lax.fori_loop(0, tm, body, 0, unroll=8)

    @pl.when(i == 0)
    def _():
        gather(dcur_ref, 0)

    @pl.when(i + 1 < pl.num_programs(0))
    def _():
        gather(dnxt_ref, 1 - slot)

    for kk in range(TOP_K):
        pltpu.make_async_copy(ybuf.at[slot, kk], ybuf.at[slot, kk], sem.at[slot, kk % 2]).wait()
    gt = gt_ref[...]
    parts = None
    for kk in range(TOP_K):
        gk = jnp.broadcast_to(gt[:, kk:kk + 1], (tm, LANES))
        yk = [gk * t for t in _load_row_tiles(ybuf, tm, (slot, kk))]
        parts = yk if parts is None else [a + b for a, b in zip(parts, yk)]
    ff = jnp.concatenate(parts, axis=1)
    xa = DEEPNORM_ALPHA * x1_ref[...] + g2_ref[...] * ff
    o_ref[...] = _layernorm(xa) * lg_ref[...] + lb_ref[...]


def _combine(grp, x1, gates, dest, y_buf, mod, lg, lb):
    tm = grp.tm
    last = grp.tiles - 1
    dblk = dest.reshape(grp.tiles, 1, tm * TOP_K)
    dspec = lambda f: pl.BlockSpec((None, 1, tm * TOP_K), f, memory_space=pltpu.SMEM)
    return pl.pallas_call(
        functools.partial(_combine_body, tm),
        grid=(grp.tiles,),
        in_specs=[dspec(lambda i: (i, 0, 0)), dspec(lambda i: (jnp.minimum(i + 1, last), 0, 0)),
                  _row_spec(grp, D_MODEL), _row_spec(grp, LANES), _mod_spec(grp, 5),
                  _const_spec(lg.shape), _const_spec(lb.shape), pl.BlockSpec(memory_space=pl.ANY)],
        out_specs=_row_spec(grp, D_MODEL),
        out_shape=jax.ShapeDtypeStruct((grp.rows, D_MODEL), F32),
        scratch_shapes=[pltpu.VMEM((2, TOP_K, tm * LANE_TILES, LANES), F32), pltpu.SemaphoreType.DMA((2, 2))],
        compiler_params=_params(1),
        name="combine_ln",
    )(dblk, dblk, x1, gates, mod, lg, lb, y_buf)


def kernel(x_prompt, x_sample, state_wkv, state_shift, state_conv, c_prompt, c_sample, w_in, tm_mu, w0, w2, a0, a2, g2, v0, v1, v2, k_k, k_a, r_k, lnx_g, lnx_b, conv_w, conv_b, conv_ln_g, conv_ln_b, w_out, ada_w, ada_b, ln1_g, ln1_b, ln2_g, ln2_b, router_w, router_b, w_gu, b_gu, w_dn, b_dn):
    bp, tp, _ = x_prompt.shape
    bs, ts, _ = x_sample.shape
    depth = w_in.shape[0]
    n_p, n_s = bp * tp, bs * ts
    n_tok = n_p + n_s
    gp = _Group(n_p, tp, 1, False)
    gs = _Group(n_s, n_s, bs, True)

    mod_all = _ada(jnp.concatenate([c_prompt, c_sample], axis=0), ada_w, ada_b)

    w_in_b = w_in.astype(BF16)
    w_out_b = w_out.astype(BF16)
    hid = jnp.arange(RWKV_DIM) // HEAD_DIM
    gmat = (hid[:, None] == hid[None, :]).astype(BF16)
    zero_wa = jnp.zeros((depth, DECAY_LORA, RWKV_DIM), F32)
    wwa = jnp.concatenate([jnp.concatenate([w2, zero_wa], axis=2),
                           jnp.concatenate([zero_wa, a2], axis=2)], axis=1).astype(BF16)
    v1p = jnp.pad(v1, ((0, 0), (0, 0), (0, LANES - VRES_LORA))).astype(BF16)
    v2p = jnp.pad(v2, ((0, 0), (0, LANES - VRES_LORA), (0, 0))).astype(BF16)
    rw_p = jnp.pad(router_w, ((0, 0), (0, 0), (0, LANES - N_EXPERTS)))
    rw_hi = rw_p.astype(BF16)
    rw_lo = (rw_p - rw_hi.astype(F32)).astype(BF16)
    rb_p = jnp.pad(router_b, ((0, 0), (0, LANES - N_EXPERTS)), constant_values=-1e30)
    b_gu4 = b_gu.reshape(depth, N_EXPERTS, 1, 2 * D_FF)
    b_dn4 = b_dn.reshape(depth, N_EXPERTS, 1, D_MODEL)

    def row2(a):
        return a.reshape(1, -1)

    x_p = x_prompt.reshape(n_p, D_MODEL)
    x_s = jnp.transpose(x_sample, (1, 0, 2)).reshape(n_s, D_MODEL)
    zeros_wkv = jnp.zeros((bp, RWKV_HEADS, HEAD_DIM, HEAD_DIM), F32)
    zeros_shift = jnp.zeros((bp, 1, SHIFT_DIM), F32)
    zeros_conv = jnp.zeros((bp, CONV_HIST, CONV_DIM), F32)
    wkv_s_t = jnp.transpose(state_wkv, (0, 2, 4, 3, 1))

    vf_p = vf_s = None
    wkv_p_out, wkv_s_out, shift_p_out, shift_s_out, conv_p_out, conv_s_out = [], [], [], [], [], []
    for l in range(depth):
        mod_p = mod_all[l, :bp].reshape(bp, 1, 6 * D_MODEL)
        mod_s = jnp.tile(mod_all[l, bp:], (ts, 1))
        wts = {"mu": row2(tm_mu[l]), "w0": row2(w0[l]), "a0": row2(a0[l]), "k_k": row2(k_k[l]),
               "k_a": row2(k_a[l]), "wwa": wwa[l], "g2": g2[l].astype(BF16),
               "lnx_g": row2(lnx_g[l]), "lnx_b": row2(lnx_b[l]), "r_k": row2(r_k[l]), "w_out": w_out_b[l],
               "ln1_g": row2(ln1_g[l]), "ln1_b": row2(ln1_b[l]), "router_w": rw_hi[l], "router_w_lo": rw_lo[l],
               "router_b": row2(rb_p[l])}
        if l > 0:
            wts.update({"v0": row2(v0[l - 1]), "v1": v1p[l - 1], "v2": v2p[l - 1]})

        per_group = []
        for grp, x, mod, is_p in ((gp, x_p, mod_p, True), (gs, x_s, mod_s, False)):
            if is_p:
                shift0, conv0 = zeros_shift, zeros_conv
            else:
                shift0 = state_shift[l][None]
                conv0 = jnp.transpose(state_conv[l], (1, 0, 2)).reshape(1, CONV_HIST * bs, CONV_DIM)
            vfirst = vf_p if is_p else vf_s
            r, lw, k, v, an, bn, g, sh_last, zc = _prep(grp, x, mod, w_in_b[l], shift0,
                                                        vfirst if l > 0 else None, wts, gmat)
            cv, tail = _conv(grp, zc, conv0, conv_w[l], row2(conv_b[l]), row2(conv_ln_g[l]), row2(conv_ln_b[l]))
            if l == 0:
                if is_p:
                    vf_p = v
                else:
                    vf_s = v
            if is_p:
                y, s_new = _wkv_chunked(bp, tp, r, lw, k, v, an, bn, zeros_wkv)
                wkv_p_out.append(s_new)
                shift_p_out.append(sh_last[:, 0])
            else:
                tmaj = lambda a: jnp.transpose(a.reshape(ts, bs, RWKV_DIM), (0, 2, 1))
                y_t, s_new = _wkv_lanes(ts, tmaj(r), tmaj(lw), tmaj(k), tmaj(v), tmaj(an), tmaj(bn), wkv_s_t[l])
                y = jnp.transpose(y_t, (0, 2, 1)).reshape(n_s, RWKV_DIM)
                wkv_s_out.append(jnp.transpose(s_new, (3, 0, 2, 1)))
                shift_s_out.append(sh_last[0])
            if is_p:
                conv_p_out.append(tail)
            else:
                conv_s_out.append(jnp.transpose(tail.reshape(CONV_HIST, bs, CONV_DIM), (1, 0, 2)))
            off = 0 if is_p else n_p
            x1, h2, ti, gt = _mix(grp, x, y, r, k, v, g, cv, mod, wts, gmat)
            per_group.append((grp, x1, h2, ti, gt, mod, off))

        top_i = jnp.concatenate([pg[3] for pg in per_group], axis=0)
        dest, block_e, n_used, n_rows = _route(top_i, gp.tm)
        if l == 0:
            x_buf = jnp.zeros((n_rows * LANE_TILES, LANES), F32)
        for grp, x1, h2, ti, gt, mod, off in per_group:
            x_buf = _dispatch(grp, h2, dest[off:off + grp.rows], x_buf)
        y_buf = _experts(l, x_buf, block_e, n_used, w_gu, b_gu4, w_dn, b_dn4)
        outs = []
        for grp, x1, h2, ti, gt, mod, off in per_group:
            outs.append(_combine(grp, x1, gt, dest[off:off + grp.rows], y_buf, mod, row2(ln2_g[l]), row2(ln2_b[l])))
        x_p, x_s = outs

    y_prompt = x_p.reshape(bp, tp, D_MODEL)
    y_sample = jnp.transpose(x_s.reshape(ts, bs, D_MODEL), (1, 0, 2))
    return (y_prompt, y_sample, jnp.stack(wkv_p_out), jnp.stack(wkv_s_out), jnp.stack(shift_p_out),
            jnp.stack(shift_s_out), jnp.stack(conv_p_out), jnp.stack(conv_s_out))
```

```python
import functools

import jax
import jax.numpy as jnp
from jax import lax
from jax.experimental import pallas as pl
from jax.experimental.pallas import tpu as pltpu

F32 = jnp.float32
BF16 = jnp.bfloat16

D_MODEL = 1024
DEPTH = 4
HEAD_DIM = 64
RWKV_DIM = 512
RWKV_HEADS = RWKV_DIM // HEAD_DIM
CONV_DIM = 512
CONV_WIDTH = 31
CONV_HIST = CONV_WIDTH - 1
DECAY_LORA = 64
AAA_LORA = 64
GATE_LORA = 128
VRES_LORA = 32
SHIFT_DIM = 3 * RWKV_DIM + DECAY_LORA + AAA_LORA + GATE_LORA
PROJ_DIM = SHIFT_DIM + 2 * CONV_DIM
N_EXPERTS = 32
TOP_K = 4
D_FF = D_MODEL
SWIGLU_LIMIT = 7.0
SWIGLU_ALPHA = 1.702
LN_EPS = 1e-5
GN_EPS = 64e-5
DEEPNORM_ALPHA = (2 * DEPTH) ** 0.25

LANES = 128
SUBLANES = 8
ROW_TILE = 512
CHUNK = 64
WKV_BATCHES_PER_STEP = 8
EXPERT_ROWS = 512
VMEM_LIMIT = 56 * 1024 * 1024


def _params(n_axes=1, vmem=VMEM_LIMIT):
    return pltpu.CompilerParams(dimension_semantics=("arbitrary",) * n_axes, vmem_limit_bytes=vmem)


def _round_up(x, m):
    return (x + m - 1) // m * m


def _bdot(a, b):
    return jnp.dot(a.astype(BF16), b.astype(BF16), preferred_element_type=F32)


def _split3(x):
    hi = x.astype(BF16)
    r1 = x - hi.astype(F32)
    mid = r1.astype(BF16)
    lo = (r1 - mid.astype(F32)).astype(BF16)
    return hi, mid, lo


def _headsum(x, g):
    return jnp.dot(x.astype(BF16), g, preferred_element_type=F32)


LANE_TILES = D_MODEL // LANES
assert LANE_TILES == SUBLANES


def _store_row_tiles(ref, x, rows, lead=()):
    for c in range(LANE_TILES):
        ref[lead + (pl.ds(c, rows, stride=LANE_TILES), slice(None))] = x[:, c * LANES:(c + 1) * LANES]


def _load_row_tiles(ref, rows, lead=()):
    return [ref[lead + (pl.ds(c, rows, stride=LANE_TILES), slice(None))] for c in range(LANE_TILES)]


def _row_tile(ref, r, lead=()):
    return ref.at[lead + (pl.ds(pl.multiple_of(r * LANE_TILES, LANE_TILES), LANE_TILES), slice(None))]


def _layernorm(x):
    mu = jnp.mean(x, axis=-1, keepdims=True)
    xc = x - mu
    var = jnp.mean(xc * xc, axis=-1, keepdims=True)
    return xc * lax.rsqrt(var + LN_EPS)


def _ada_body(c_ref, w_ref, b_ref, o_ref):
    c = c_ref[...]
    cs = c * jax.nn.sigmoid(c)
    o_ref[...] = _bdot(cs, w_ref[...]) + b_ref[...]


def _ada(c_all, ada_w, ada_b):
    depth, _, width = ada_w.shape
    nb = c_all.shape[0]
    tn = 1536
    return pl.pallas_call(
        _ada_body,
        grid=(depth, width // tn),
        in_specs=[pl.BlockSpec((nb, D_MODEL), lambda l, j: (0, 0)),
                  pl.BlockSpec((None, D_MODEL, tn), lambda l, j: (l, 0, j)),
                  pl.BlockSpec((None, 1, tn), lambda l, j: (l, 0, j))],
        out_specs=pl.BlockSpec((None, nb, tn), lambda l, j: (l, 0, j)),
        out_shape=jax.ShapeDtypeStruct((depth, nb, width), F32),
        compiler_params=_params(2),
        name="ada_mod",
    )(c_all, ada_w, ada_b.reshape(depth, 1, width))


class _Group:
    def __init__(self, rows, seg_len, stride, mod_per_row):
        self.rows = rows
        self.seg_len = seg_len
        self.stride = stride
        self.mod_per_row = mod_per_row
        self.tm = min(ROW_TILE, rows)
        assert rows % self.tm == 0 and seg_len % self.tm == 0
        self.tiles = rows // self.tm
        self.tiles_per_seg = seg_len // self.tm
        self.nseg = rows // seg_len


def _mod_spec(grp, which):
    if grp.mod_per_row:
        return pl.BlockSpec((grp.tm, D_MODEL), lambda i: (i, which))
    tps = grp.tiles_per_seg
    return pl.BlockSpec((None, 1, D_MODEL), lambda i: (i // tps, 0, which))


def _row_spec(grp, width):
    return pl.BlockSpec((grp.tm, width), lambda i: (i, 0))


def _const_spec(shape):
    nd = len(shape)
    return pl.BlockSpec(shape, lambda i: (0,) * nd)


def _prep_body(has_vres, grp, halo, *refs):
    it = iter(refs)
    x_ref, sc_ref, sh_ref, win_ref, sh0_ref = (next(it) for _ in range(5))
    vf_ref = next(it) if has_vres else None
    mu_ref, w0_ref, a0_ref, kk_ref, ka_ref, wwa_ref, g2_ref = (next(it) for _ in range(7))
    if has_vres:
        v0_ref, v1_ref, v2_ref = next(it), next(it), next(it)
    gmat_ref = next(it)
    r_o, lw_o, k_o, v_o, an_o, bn_o, g_o, shl_o, zc_o = (next(it) for _ in range(9))
    ext = next(it)

    tm, s = grp.tm, grp.stride
    j = pl.program_id(0) % grp.tiles_per_seg

    @pl.when(j == 0)
    def _():
        ext[pl.ds(halo - s, s), :] = sh0_ref[...]

    h = x_ref[...] * (1.0 + sc_ref[...]) + sh_ref[...]
    z = jnp.dot(h.astype(BF16), win_ref[...], preferred_element_type=F32)
    zc_o[...] = z[:, SHIFT_DIM:]
    zs = z[:, :SHIFT_DIM]
    ext[pl.ds(halo, tm), :] = zs
    prev = ext[pl.ds(halo - s, tm), :]
    shl_o[...] = ext[pl.ds(halo + tm - s, s), :]
    if grp.tiles_per_seg > 1:
        ext[pl.ds(0, halo), :] = ext[pl.ds(tm, halo), :]

    zm = zs + (prev - zs) * mu_ref[...]
    i1, i2, i3 = RWKV_DIM, 2 * RWKV_DIM, 3 * RWKV_DIM
    i5 = i3 + DECAY_LORA + AAA_LORA
    r, k, v = zm[:, :i1], zm[:, i1:i2], zm[:, i2:i3]
    zwa, zg = zm[:, i3:i5], zm[:, i5:]
    lane = lax.broadcasted_iota(jnp.int32, zwa.shape, 1)
    xwa = jnp.where(lane < DECAY_LORA, jnp.tanh(zwa), zwa)
    wa = _bdot(xwa, wwa_ref[...])
    w_in = w0_ref[...] + wa[:, :RWKV_DIM]
    neg = -w_in
    softplus = jnp.maximum(neg, 0.0) + jnp.log(1.0 + jnp.exp(-jnp.abs(neg)))
    w_log = -softplus - 0.5
    lw = -jnp.exp(w_log)
    a = jax.nn.sigmoid(a0_ref[...] + wa[:, RWKV_DIM:])
    g = _bdot(jax.nn.sigmoid(zg), g2_ref[...])
    if has_vres:
        vl = _bdot(_bdot(v, v1_ref[...]), v2_ref[...])
        vg = jax.nn.sigmoid(v0_ref[...] + vl)
        v = v + (vf_ref[...] - v) * vg
    kk = k * kk_ref[...]
    ss = _headsum(kk * kk, gmat_ref[...])
    kk = kk * lax.rsqrt(jnp.maximum(ss, 1e-24))
    k = k * (1.0 + (a - 1.0) * ka_ref[...])
    r_o[...] = r
    lw_o[...] = lw
    k_o[...] = k
    v_o[...] = v
    an_o[...] = -kk
    bn_o[...] = kk * a
    g_o[...] = g


def _prep(grp, x, mod, w_in_b, shift0, vfirst, wts, gmat):
    has_vres = vfirst is not None
    halo = _round_up(grp.stride, SUBLANES)
    s = grp.stride
    tps = grp.tiles_per_seg
    seg_spec = pl.BlockSpec((None, s, SHIFT_DIM), lambda i: (i // tps, 0, 0))
    in_specs = [_row_spec(grp, D_MODEL), _mod_spec(grp, 1), _mod_spec(grp, 0),
                _const_spec((D_MODEL, PROJ_DIM)), seg_spec]
    args = [x, mod, mod, w_in_b, shift0]
    if has_vres:
        in_specs.append(_row_spec(grp, RWKV_DIM))
        args.append(vfirst)
    names = ["mu", "w0", "a0", "k_k", "k_a", "wwa", "g2"] + (["v0", "v1", "v2"] if has_vres else [])
    for nm in names:
        in_specs.append(_const_spec(wts[nm].shape))
        args.append(wts[nm])
    in_specs.append(_const_spec(gmat.shape))
    args.append(gmat)
    out = jax.ShapeDtypeStruct((grp.rows, RWKV_DIM), F32)
    return pl.pallas_call(
        functools.partial(_prep_body, has_vres, grp, halo),
        grid=(grp.tiles,),
        in_specs=in_specs,
        out_specs=[_row_spec(grp, RWKV_DIM)] * 7 + [seg_spec, _row_spec(grp, 2 * CONV_DIM)],
        out_shape=[out] * 7 + [jax.ShapeDtypeStruct((grp.nseg, s, SHIFT_DIM), F32),
                               jax.ShapeDtypeStruct((grp.rows, 2 * CONV_DIM), F32)],
        scratch_shapes=[pltpu.VMEM((halo + grp.tm, SHIFT_DIM), F32)],
        compiler_params=_params(1),
        name="inproj_prep",
    )(*args)


def _nt(a, b):
    return lax.dot_general(a.astype(BF16), b.astype(BF16), (((1,), (1,)), ((), ())),
                           preferred_element_type=F32)


def _tn(a, b):
    return lax.dot_general(a.astype(BF16), b.astype(BF16), (((0,), (0,)), ((), ())),
                           preferred_element_type=F32)


def _wkv_chunk_body(nchunks, nb, r_ref, lw_ref, k_ref, v_ref, a_ref, b_ref, s0_ref, y_ref, so_ref, st):
    c = pl.program_id(1)
    L, N = CHUNK, HEAD_DIM

    @pl.when(c == 0)
    def _():
        st[...] = s0_ref[...]

    P2, W2 = 2 * L, 2 * N
    assert L == N and W2 == LANES
    trow = lax.broadcasted_iota(jnp.int32, (L, L), 0)
    tcol = lax.broadcasted_iota(jnp.int32, (L, L), 1)
    tri = jnp.where(trow >= tcol, 1.0, 0.0).astype(BF16)
    row = lax.broadcasted_iota(jnp.int32, (P2, P2), 0)
    col = lax.broadcasted_iota(jnp.int32, (P2, P2), 1)
    lg = L.bit_length() - 1
    same_head = (row >> lg) == (col >> lg)
    strict = same_head & (row > col)
    incl = same_head & (row >= col)
    eye = row == col
    sh = 3
    diag_blk = (row >> sh) == (col >> sh)
    off_masks = []
    while (1 << sh) < L:
        off_masks.append(((row >> sh) == (col >> sh) + 1) & ((row >> (sh + 1)) == (col >> (sh + 1))))
        sh += 1

    lane_lo = lax.broadcasted_iota(jnp.int32, (L, W2), 1) < N
    npair = RWKV_HEADS // 2

    def stack(x, p):
        xp = x[:, p * W2:(p + 1) * W2]
        return jnp.concatenate([jnp.where(lane_lo, xp, 0.0), jnp.where(lane_lo, 0.0, xp)], axis=0)

    at, rt, bt, kt, bh, kh, vv, p_last = [], [], [], [], [], [], [], []
    for bi in range(nb):
        lw = lw_ref[bi]
        h3 = _split3(lw)
        cl = (jnp.dot(tri, h3[0], preferred_element_type=F32) + jnp.dot(tri, h3[1], preferred_element_type=F32)
              + jnp.dot(tri, h3[2], preferred_element_type=F32))
        cl_last = cl[L - 1:L, :]
        ecl = jnp.exp(cl)
        encl = jnp.exp(-cl)
        dl = jnp.exp(cl_last - cl)
        pl_row = jnp.exp(cl_last)
        an, bn, kx, rx, vx = a_ref[bi], b_ref[bi], k_ref[bi], r_ref[bi], v_ref[bi]
        at_all, rt_all = an * jnp.exp(cl - lw), rx * ecl
        bt_all, kt_all, bh_all, kh_all = bn * encl, kx * encl, bn * dl, kx * dl
        for p in range(npair):
            at.append(stack(at_all, p).astype(BF16))
            rt.append(stack(rt_all, p))
            bt.append(stack(bt_all, p).astype(BF16))
            kt.append(stack(kt_all, p).astype(BF16))
            bh.append(stack(bh_all, p).astype(BF16))
            kh.append(stack(kh_all, p).astype(BF16))
            vv.append(stack(vx, p).astype(BF16))
            p_last.append(pl_row[:, p * W2:(p + 1) * W2])
    pairs = range(nb * npair)

    ar = [jnp.concatenate([at[p], rt[p].astype(BF16)], axis=0) for p in pairs]
    xb = [_nt(ar[p], bt[p]) for p in pairs]
    xk = [_nt(ar[p], kt[p]) for p in pairs]
    a_ab = [jnp.where(strict, xb[p][:P2], 0.0) for p in pairs]
    a_ak = [jnp.where(strict, xk[p][:P2], 0.0) for p in pairs]
    a_rb = [jnp.where(incl, xb[p][P2:], 0.0).astype(BF16) for p in pairs]
    a_rk = [jnp.where(incl, xk[p][P2:], 0.0) for p in pairs]
    pw = [jnp.where(diag_blk, a_ab[p], 0.0) for p in pairs]
    tm = [jnp.where(eye, 1.0, 0.0) + pw[p] for p in pairs]
    for _ in range(2):
        pw = [_bdot(pw[p], pw[p]) for p in pairs]
        tm = [tm[p] + _bdot(tm[p], pw[p]) for p in pairs]
    for off in off_masks:
        t1 = [_bdot(jnp.where(off, a_ab[p], 0.0), tm[p]) for p in pairs]
        tm = [tm[p] + _bdot(tm[p], t1[p]) for p in pairs]
    akv = [_bdot(a_ak[p], vv[p]).astype(BF16) for p in pairs]
    wu = [_bdot(tm[p], jnp.concatenate([at[p], akv[p]], axis=1)) for p in pairs]
    qy = [jnp.dot(a_rb[p], wu[p].astype(BF16), preferred_element_type=F32) for p in pairs]
    rkv = [_bdot(a_rk[p], vv[p]) for p in pairs]
    ys = []
    for p in pairs:
        bi, pp = divmod(p, npair)
        s_p = st[bi, pp]
        w, u0 = wu[p][:, :W2], wu[p][:, W2:]
        q = rt[p] + qy[p][:, :W2]
        ybd = _nt(q, s_p) + qy[p][:, W2:] + rkv[p]
        ys.append(ybd[:L] + ybd[L:])
        cm = _tn(u0, bh[p]) + _tn(vv[p], kh[p])
        st[bi, pp] = s_p * p_last[p] + _bdot(s_p, _tn(w, bh[p])) + cm
    for bi in range(nb):
        y_ref[bi] = jnp.concatenate(ys[bi * npair:(bi + 1) * npair], axis=1)

    @pl.when(c == nchunks - 1)
    def _():
        so_ref[...] = st[...]


def _wkv_chunked(nbatch, tlen, r, lw, k, v, an, bn, s0):
    nchunks = tlen // CHUNK
    npair, w2 = RWKV_HEADS // 2, 2 * HEAD_DIM
    s5 = s0.reshape(nbatch, npair, 2, HEAD_DIM, HEAD_DIM)
    zero = jnp.zeros_like(s5[:, :, 0])
    s_bd = jnp.concatenate([jnp.concatenate([s5[:, :, 0], zero], axis=-1),
                            jnp.concatenate([zero, s5[:, :, 1]], axis=-1)], axis=-2)
    nb = WKV_BATCHES_PER_STEP if nbatch % WKV_BATCHES_PER_STEP == 0 else 1
    tok = pl.BlockSpec((nb, CHUNK, RWKV_DIM), lambda b, c: (b, c, 0))
    stt = pl.BlockSpec((nb, npair, w2, w2), lambda b, c: (b, 0, 0, 0))
    seq = lambda t: t.reshape(nbatch, tlen, RWKV_DIM)
    y, so = pl.pallas_call(
        functools.partial(_wkv_chunk_body, nchunks, nb),
        grid=(nbatch // nb, nchunks),
        in_specs=[tok] * 6 + [stt],
        out_specs=[tok, stt],
        out_shape=[jax.ShapeDtypeStruct((nbatch, tlen, RWKV_DIM), F32),
                   jax.ShapeDtypeStruct((nbatch, npair, w2, w2), F32)],
        scratch_shapes=[pltpu.VMEM((nb, npair, w2, w2), F32)],
        compiler_params=_params(2),
        name="wkv_chunked",
    )(seq(r), seq(lw), seq(k), seq(v), seq(an), seq(bn), s_bd)
    s_new = jnp.stack([so[:, :, :HEAD_DIM, :HEAD_DIM], so[:, :, HEAD_DIM:, HEAD_DIM:]], axis=2)
    return y.reshape(nbatch * tlen, RWKV_DIM), s_new.reshape(nbatch, RWKV_HEADS, HEAD_DIM, HEAD_DIM)


def _wkv_lane_body(tlen, r_ref, lw_ref, k_ref, v_ref, a_ref, b_ref, s_ref, y_ref, so_ref):
    N = HEAD_DIM
    nb = s_ref.shape[-1]

    def row(ref, t, j):
        return jnp.broadcast_to(ref[t, pl.ds(j, 1), :], (N, nb))

    def first(j, acc):
        return acc + s_ref[j] * row(a_ref, 0, j)

    sa = lax.fori_loop(0, N, first, jnp.zeros((N, nb), F32))
    for t in range(tlen):
        v_t = v_ref[t]
        src = s_ref if t == 0 else so_ref

        def body(j, carry, t=t, v_t=v_t, src=src, sa=sa):
            y, sa_next = carry
            sj = src[j] * jnp.exp(row(lw_ref, t, j)) + sa * row(b_ref, t, j) + v_t * row(k_ref, t, j)
            so_ref[j] = sj
            y = y + sj * row(r_ref, t, j)
            if t + 1 < tlen:
                sa_next = sa_next + sj * row(a_ref, t + 1, j)
            return y, sa_next

        zero = jnp.zeros((N, nb), F32)
        y, sa = lax.fori_loop(0, N, body, (zero, zero))
        y_ref[t] = y


def _wkv_lanes(tlen, r, lw, k, v, an, bn, s_t):
    nb = r.shape[-1]
    tok = pl.BlockSpec((tlen, HEAD_DIM, nb), lambda h: (0, h, 0))
    stt = pl.BlockSpec((None, HEAD_DIM, HEAD_DIM, nb), lambda h: (h, 0, 0, 0))
    return pl.pallas_call(
        functools.partial(_wkv_lane_body, tlen),
        grid=(RWKV_HEADS,),
        in_specs=[tok] * 6 + [stt],
        out_specs=[tok, stt],
        out_shape=[jax.ShapeDtypeStruct((tlen, RWKV_DIM, nb), F32),
                   jax.ShapeDtypeStruct((RWKV_HEADS, HEAD_DIM, HEAD_DIM, nb), F32)],
        compiler_params=_params(1),
        name="wkv_lanes",
    )(r, lw, k, v, an, bn, s_t)


def _conv_body(grp, halo, zc_ref, c0_ref, cw_ref, cb_ref, lg_ref, lb_ref, cv_ref, tail_ref, ext, win):
    tm, s = grp.tm, grp.stride
    hist = CONV_HIST * s
    j = pl.program_id(0) % grp.tiles_per_seg

    @pl.when(j == 0)
    def _():
        ext[pl.ds(halo - hist, hist), :] = c0_ref[...]

    zc = zc_ref[...]
    u = zc[:, :CONV_DIM] * jax.nn.sigmoid(zc[:, CONV_DIM:])
    ext[pl.ds(halo, tm), :] = u
    acc = jnp.broadcast_to(cb_ref[...], (tm, CONV_DIM))
    offs = [halo - hist + w * s for w in range(CONV_WIDTH)]
    for phase in range(SUBLANES):
        taps = [w for w in range(CONV_WIDTH) if offs[w] % SUBLANES == phase]
        if not taps:
            continue
        if s % SUBLANES == 0:
            for w in taps:
                acc = acc + cw_ref[pl.ds(w, 1), :] * ext[pl.ds(offs[w], tm), :]
            continue
        lo = offs[taps[0]]
        span = tm + offs[taps[-1]] - lo
        win[pl.ds(0, span), :] = ext[pl.ds(lo, span), :]
        for w in taps:
            acc = acc + cw_ref[pl.ds(w, 1), :] * win[pl.ds(offs[w] - lo, tm), :]
    y = _layernorm(acc) * lg_ref[...] + lb_ref[...]
    cv_ref[...] = y * jax.nn.sigmoid(y)
    tail_ref[...] = ext[pl.ds(halo + tm - hist, hist), :]
    if grp.tiles_per_seg > 1:
        ext[pl.ds(0, halo), :] = ext[pl.ds(tm, halo), :]


def _conv(grp, zc, conv0, cw, cb, lg, lb):
    hist = CONV_HIST * grp.stride
    halo = _round_up(hist, SUBLANES)
    tps = grp.tiles_per_seg
    hist_spec = pl.BlockSpec((None, hist, CONV_DIM), lambda i: (i // tps, 0, 0))
    return pl.pallas_call(
        functools.partial(_conv_body, grp, halo),
        grid=(grp.tiles,),
        in_specs=[_row_spec(grp, 2 * CONV_DIM), hist_spec,
                  _const_spec(cw.shape), _const_spec(cb.shape), _const_spec(lg.shape), _const_spec(lb.shape)],
        out_specs=[_row_spec(grp, CONV_DIM), hist_spec],
        out_shape=[jax.ShapeDtypeStruct((grp.rows, CONV_DIM), F32),
                   jax.ShapeDtypeStruct((grp.nseg, hist, CONV_DIM), F32)],
        scratch_shapes=[pltpu.VMEM((halo + grp.tm, CONV_DIM), F32),
                        pltpu.VMEM((grp.tm + _round_up(CONV_HIST, SUBLANES), CONV_DIM), F32)],
        compiler_params=_params(1),
        name="conv_group",
    )(zc, conv0, cw, cb, lg, lb)


def _mix_body(x_ref, y_ref, r_ref, k_ref, v_ref, g_ref, cv_ref, g1_ref, sc2_ref, sh2_ref,
              lnxg_ref, lnxb_ref, rk_ref, wout_ref, l1g_ref, l1b_ref, rw_ref, rwl_ref, rb_ref, gmat_ref,
              x1_ref, h2_ref, ti_ref, gt_ref):
    gm = gmat_ref[...]
    y = y_ref[...]
    inv = 1.0 / HEAD_DIM
    ym = _headsum(y, gm) * inv
    yc = y - ym
    yv = _headsum(yc * yc, gm) * inv
    yn = yc * lax.rsqrt(yv + GN_EPS) * lnxg_ref[...] + lnxb_ref[...]
    bonus = _headsum(r_ref[...] * k_ref[...] * rk_ref[...], gm) * v_ref[...]
    yo = (yn + bonus) * g_ref[...]
    mix = (jnp.dot(yo.astype(BF16), wout_ref[pl.ds(0, RWKV_DIM), :], preferred_element_type=F32)
           + jnp.dot(cv_ref[...].astype(BF16), wout_ref[pl.ds(RWKV_DIM, CONV_DIM), :],
                     preferred_element_type=F32))
    xa = DEEPNORM_ALPHA * x_ref[...] + g1_ref[...] * mix
    x1 = _layernorm(xa) * l1g_ref[...] + l1b_ref[...]
    x1_ref[...] = x1
    h2 = x1 * (1.0 + sc2_ref[...]) + sh2_ref[...]
    _store_row_tiles(h2_ref, h2, h2.shape[0])
    h_hi = h2.astype(BF16)
    h_lo = (h2 - h_hi.astype(F32)).astype(BF16)
    logits = (jnp.dot(h_hi, rw_ref[...], preferred_element_type=F32)
              + jnp.dot(h_hi, rwl_ref[...], preferred_element_type=F32)
              + jnp.dot(h_lo, rw_ref[...], preferred_element_type=F32)) + rb_ref[...]
    lane = lax.broadcasted_iota(jnp.int32, logits.shape, 1).astype(F32)
    cur = logits
    vals, idxs = [], []
    for _ in range(TOP_K):
        m = jnp.max(cur, axis=-1, keepdims=True)
        idx = jnp.min(jnp.where(cur == m, lane, float(LANES)), axis=-1, keepdims=True)
        vals.append(m)
        idxs.append(idx)
        cur = jnp.where(lane == idx, -jnp.inf, cur)
    es = [jnp.exp(vv - vals[0]) for vv in vals]
    den = es[0] + es[1] + es[2] + es[3]
    ti = jnp.zeros_like(logits)
    gt = jnp.zeros_like(logits)
    for kk in range(TOP_K):
        ti = jnp.where(lane == float(kk), idxs[kk], ti)
        gt = jnp.where(lane == float(kk), es[kk] / den, gt)
    ti_ref[...] = ti.astype(jnp.int32)
    gt_ref[...] = gt


def _mix(grp, x, y, r, k, v, g, cv, mod, wts, gmat):
    half = _row_spec(grp, RWKV_DIM)
    names = ["lnx_g", "lnx_b", "r_k", "w_out", "ln1_g", "ln1_b", "router_w", "router_w_lo", "router_b"]
    in_specs = ([_row_spec(grp, D_MODEL)] + [half] * 6 + [_mod_spec(grp, 2), _mod_spec(grp, 4), _mod_spec(grp, 3)]
                + [_const_spec(wts[nm].shape) for nm in names] + [_const_spec(gmat.shape)])
    args = [x, y, r, k, v, g, cv, mod, mod, mod] + [wts[nm] for nm in names] + [gmat]
    return pl.pallas_call(
        _mix_body,
        grid=(grp.tiles,),
        in_specs=in_specs,
        out_specs=[_row_spec(grp, D_MODEL), pl.BlockSpec((grp.tm * LANE_TILES, LANES), lambda i: (i, 0)),
                   _row_spec(grp, LANES), _row_spec(grp, LANES)],
        out_shape=[jax.ShapeDtypeStruct((grp.rows, D_MODEL), F32),
                   jax.ShapeDtypeStruct((grp.rows * LANE_TILES, LANES), F32),
                   jax.ShapeDtypeStruct((grp.rows, LANES), jnp.int32),
                   jax.ShapeDtypeStruct((grp.rows, LANES), F32)],
        compiler_params=_params(1),
        name="mix_out",
    )(*args)


def _route_body(ti_ref, dest_ref, cnt_ref, carry, pstart):
    phase = pl.program_id(0)
    i = pl.program_id(1)
    ti = ti_ref[...]
    tm = ti.shape[0]
    lane = lax.broadcasted_iota(jnp.int32, (tm, LANES), 1)
    sel = [lane == ti[:, kk:kk + 1] for kk in range(TOP_K)]
    onehot = jnp.zeros((tm, LANES), F32)
    for kk in range(TOP_K):
        onehot = onehot + jnp.where(sel[kk], 1.0, 0.0)

    @pl.when((phase == 0) & (i == 0))
    def _():
        carry[...] = jnp.zeros(carry.shape, F32)

    @pl.when((phase == 1) & (i == 0))
    def _():
        cnt = carry[...]
        cnt_ref[...] = cnt
        padded = jnp.floor((cnt + (EXPERT_ROWS - 1)) * (1.0 / EXPERT_ROWS)) * EXPERT_ROWS
        er = lax.broadcasted_iota(jnp.int32, (LANES, LANES), 0)
        ec = lax.broadcasted_iota(jnp.int32, (LANES, LANES), 1)
        earlier = jnp.where(er < ec, 1.0, 0.0).astype(BF16)
        h3 = _split3(jnp.broadcast_to(padded, (SUBLANES, LANES)))
        acc = (jnp.dot(h3[0], earlier, preferred_element_type=F32)
               + jnp.dot(h3[1], earlier, preferred_element_type=F32)
               + jnp.dot(h3[2], earlier, preferred_element_type=F32))
        pstart[...] = acc[0:1, :]
        carry[...] = jnp.zeros(carry.shape, F32)

    @pl.when(phase == 1)
    def _():
        rr = lax.broadcasted_iota(jnp.int32, (tm, tm), 0)
        cc = lax.broadcasted_iota(jnp.int32, (tm, tm), 1)
        before = jnp.where(rr > cc, 1.0, 0.0).astype(BF16)
        cum = jnp.dot(before, onehot.astype(BF16), preferred_element_type=F32) + (carry[...] + pstart[...])
        dest = jnp.zeros((tm, LANES), F32)
        for kk in range(TOP_K):
            pk = jnp.sum(jnp.where(sel[kk], cum, 0.0), axis=-1, keepdims=True)
            dest = jnp.where(lane == kk, pk, dest)
        dest_ref[...] = dest.astype(jnp.int32)

    carry[...] = carry[...] + jnp.sum(onehot, axis=0, keepdims=True)


def _route(top_i, tm):
    n_tok = top_i.shape[0]
    rows = EXPERT_ROWS
    dest, cnt = pl.pallas_call(
        _route_body,
        grid=(2, n_tok // tm),
        in_specs=[pl.BlockSpec((tm, LANES), lambda ph, i: (i, 0))],
        out_specs=[pl.BlockSpec((tm, LANES), lambda ph, i: (i * ph, 0)),
                   pl.BlockSpec((1, LANES), lambda ph, i: (0, 0))],
        out_shape=[jax.ShapeDtypeStruct((n_tok, LANES), jnp.int32), jax.ShapeDtypeStruct((1, LANES), F32)],
        scratch_shapes=[pltpu.VMEM((1, LANES), F32), pltpu.VMEM((1, LANES), F32)],
        compiler_params=_params(2),
        name="route_rank",
    )(top_i)
    n_blocks = -(-(n_tok * TOP_K) // rows) + N_EXPERTS
    counts = cnt[0, :N_EXPERTS].astype(jnp.int32)
    padded_end = jnp.cumsum((counts + rows - 1) // rows * rows)
    block_start = jnp.arange(n_blocks, dtype=jnp.int32) * rows
    block_e = jnp.sum((padded_end[None, :] <= block_start[:, None]).astype(jnp.int32), axis=1)
    block_e = jnp.minimum(block_e, N_EXPERTS - 1)
    n_used = (padded_end[-1] // rows).astype(jnp.int32).reshape(1)
    return dest[:, :TOP_K], block_e, n_used, n_blocks * rows


def _dispatch_body(tm, dst_ref, h_ref, xin_ref, x_hbm, sem):
    del xin_ref

    def body(r, carry):
        for kk in range(TOP_K):
            d = dst_ref[0, r * TOP_K + kk]
            pltpu.make_async_copy(_row_tile(h_ref, r), _row_tile(x_hbm, d), sem.at[kk % 2]).start(priority=kk % 2)
        return carry

    lax.fori_loop(0, tm, body, 0, unroll=8)
    for kk in range(TOP_K):
        pltpu.make_async_copy(h_ref, h_ref, sem.at[kk % 2]).wait()


def _dispatch(grp, h2, dest, x_buf):
    tm = grp.tm
    dblk = dest.reshape(grp.tiles, 1, tm * TOP_K)
    return pl.pallas_call(
        functools.partial(_dispatch_body, tm),
        grid=(grp.tiles,),
        in_specs=[pl.BlockSpec((None, 1, tm * TOP_K), lambda i: (i, 0, 0), memory_space=pltpu.SMEM),
                  pl.BlockSpec((tm * LANE_TILES, LANES), lambda i: (i, 0)), pl.BlockSpec(memory_space=pl.ANY)],
        out_specs=pl.BlockSpec(memory_space=pl.ANY),
        out_shape=jax.ShapeDtypeStruct(x_buf.shape, F32),
        scratch_shapes=[pltpu.SemaphoreType.DMA((2,))],
        input_output_aliases={2: 0},
        compiler_params=pltpu.CompilerParams(dimension_semantics=("arbitrary",), vmem_limit_bytes=VMEM_LIMIT,
                                             has_side_effects=True),
        name="dispatch",
    )(dblk, h2, x_buf)


def _expert_body(be_ref, nu_ref, gu_ref, dn_ref, x_ref, wgu_ref, bgu_ref, wdn_ref, bdn_ref, o_ref, wgu_b, wdn_b):
    del gu_ref, dn_ref
    i = pl.program_id(0)

    @pl.when(i < nu_ref[0])
    def _():
        prev_e = be_ref[jnp.maximum(i - 1, 0)]

        @pl.when((i == 0) | (be_ref[i] != prev_e))
        def _():
            wgu_b[...] = wgu_ref[...].astype(BF16)
            wdn_b[...] = wdn_ref[...].astype(BF16)

        x = jnp.concatenate(_load_row_tiles(x_ref, EXPERT_ROWS), axis=1).astype(BF16)
        gu = jnp.dot(x, wgu_b[...], preferred_element_type=F32) + bgu_ref[...]
        gate = jnp.minimum(gu[:, :D_FF], SWIGLU_LIMIT)
        lin = jnp.clip(gu[:, D_FF:], -SWIGLU_LIMIT, SWIGLU_LIMIT)
        act = gate * jax.nn.sigmoid(SWIGLU_ALPHA * gate) * (lin + 1.0)
        out = jnp.dot(act.astype(BF16), wdn_b[...], preferred_element_type=F32) + bdn_ref[...]
        _store_row_tiles(o_ref, out, EXPERT_ROWS)

    @pl.when(i >= nu_ref[0])
    def _():
        o_ref[...] = jnp.zeros(o_ref.shape, F32)


def _experts(layer, x_buf, block_e, n_used, w_gu, b_gu, w_dn, b_dn):
    rows = EXPERT_ROWS
    n_blocks = x_buf.shape[0] // (rows * LANE_TILES)
    xblk = (rows * LANE_TILES, LANES)

    idx = jnp.arange(n_blocks, dtype=jnp.int32)
    be = block_e[jnp.minimum(idx, n_used[0] - 1)]
    first = jnp.concatenate([jnp.ones((1,), bool), be[1:] != be[:-1]])
    run_start = lax.cummax(jnp.where(first, idx, 0), axis=0)
    first_at_or_after = jnp.flip(lax.cummin(jnp.flip(jnp.where(first, idx, n_blocks)), axis=0))
    next_first = jnp.concatenate([first_at_or_after[1:], jnp.full((1,), n_blocks, jnp.int32)])
    next_e = jnp.where(next_first < n_blocks, be[jnp.minimum(next_first, n_blocks - 1)], be)
    gu_idx = jnp.where(first, be, next_e)
    dn_idx = jnp.where(first | (idx == run_start + 1), be, next_e)

    grid_spec = pltpu.PrefetchScalarGridSpec(
        num_scalar_prefetch=4,
        grid=(n_blocks,),
        in_specs=[pl.BlockSpec(xblk, lambda i, be, nu, gu, dn: (jnp.minimum(i, nu[0] - 1), 0)),
                  pl.BlockSpec((None, None, D_MODEL, 2 * D_FF), lambda i, be, nu, gu, dn: (layer, gu[i], 0, 0)),
                  pl.BlockSpec((None, None, 1, 2 * D_FF), lambda i, be, nu, gu, dn: (layer, be[i], 0, 0)),
                  pl.BlockSpec((None, None, D_FF, D_MODEL), lambda i, be, nu, gu, dn: (layer, dn[i], 0, 0)),
                  pl.BlockSpec((None, None, 1, D_MODEL), lambda i, be, nu, gu, dn: (layer, be[i], 0, 0))],
        out_specs=pl.BlockSpec(xblk, lambda i, be, nu, gu, dn: (i, 0)),
        scratch_shapes=[pltpu.VMEM((D_MODEL, 2 * D_FF), BF16), pltpu.VMEM((D_FF, D_MODEL), BF16)])
    return pl.pallas_call(
        _expert_body,
        grid_spec=grid_spec,
        out_shape=jax.ShapeDtypeStruct(x_buf.shape, F32),
        compiler_params=_params(1),
        name="experts",
    )(be, n_used, gu_idx, dn_idx, x_buf, w_gu, b_gu, w_dn, b_dn)


def _combine_body(tm, dcur_ref, dnxt_ref, x1_ref, gt_ref, g2_ref, lg_ref, lb_ref, y_hbm, o_ref, ybuf, sem):
    i = pl.program_id(0)
    slot = i % 2

    def gather(dref, sl):
        def body(r, carry):
            for kk in range(TOP_K):
                d = dref[0, r * TOP_K + kk]
                pltpu.make_async_copy(_row_tile(y_hbm, d), _row_tile(ybuf, r, (sl, kk)),
                                      sem.at[sl, kk % 2]).start(priority=kk % 2)
            return carry
        lax.fori_loop(0, tm, body, 0, unroll=8)

    @pl.when(i == 0)
    def _():
        gather(dcur_ref, 0)

    @pl.when(i + 1 < pl.num_programs(0))
    def _():
        gather(dnxt_ref, 1 - slot)

    for kk in range(TOP_K):
        pltpu.make_async_copy(ybuf.at[slot, kk], ybuf.at[slot, kk], sem.at[slot, kk % 2]).wait()
    gt = gt_ref[...]
    parts = None
    for kk in range(TOP_K):
        gk = jnp.broadcast_to(gt[:, kk:kk + 1], (tm, LANES))
        yk = [gk * t for t in _load_row_tiles(ybuf, tm, (slot, kk))]
        parts = yk if parts is None else [a + b for a, b in zip(parts, yk)]
    ff = jnp.concatenate(parts, axis=1)
    xa = DEEPNORM_ALPHA * x1_ref[...] + g2_ref[...] * ff
    o_ref[...] = _layernorm(xa) * lg_ref[...] + lb_ref[...]


def _combine(grp, x1, gates, dest, y_buf, mod, lg, lb):
    tm = grp.tm
    last = grp.tiles - 1
    dblk = dest.reshape(grp.tiles, 1, tm * TOP_K)
    dspec = lambda f: pl.BlockSpec((None, 1, tm * TOP_K), f, memory_space=pltpu.SMEM)
    return pl.pallas_call(
        functools.partial(_combine_body, tm),
        grid=(grp.tiles,),
        in_specs=[dspec(lambda i: (i, 0, 0)), dspec(lambda i: (jnp.minimum(i + 1, last), 0, 0)),
                  _row_spec(grp, D_MODEL), _row_spec(grp, LANES), _mod_spec(grp, 5),
                  _const_spec(lg.shape), _const_spec(lb.shape), pl.BlockSpec(memory_space=pl.ANY)],
        out_specs=_row_spec(grp, D_MODEL),
        out_shape=jax.ShapeDtypeStruct((grp.rows, D_MODEL), F32),
        scratch_shapes=[pltpu.VMEM((2, TOP_K, tm * LANE_TILES, LANES), F32), pltpu.SemaphoreType.DMA((2, 2))],
        compiler_params=_params(1),
        name="combine_ln",
    )(dblk, dblk, x1, gates, mod, lg, lb, y_buf)


def kernel(x_prompt, x_sample, state_wkv, state_shift, state_conv, c_prompt, c_sample, w_in, tm_mu, w0, w2, a0, a2, g2, v0, v1, v2, k_k, k_a, r_k, lnx_g, lnx_b, conv_w, conv_b, conv_ln_g, conv_ln_b, w_out, ada_w, ada_b, ln1_g, ln1_b, ln2_g, ln2_b, router_w, router_b, w_gu, b_gu, w_dn, b_dn):
    bp, tp, _ = x_prompt.shape
    bs, ts, _ = x_sample.shape
    depth = w_in.shape[0]
    n_p, n_s = bp * tp, bs * ts
    n_tok = n_p + n_s
    gp = _Group(n_p, tp, 1, False)
    gs = _Group(n_s, n_s, bs, True)

    mod_all = _ada(jnp.concatenate([c_prompt, c_sample], axis=0), ada_w, ada_b)

    w_in_b = w_in.astype(BF16)
    w_out_b = w_out.astype(BF16)
    hid = jnp.arange(RWKV_DIM) // HEAD_DIM
    gmat = (hid[:, None] == hid[None, :]).astype(BF16)
    zero_wa = jnp.zeros((depth, DECAY_LORA, RWKV_DIM), F32)
    wwa = jnp.concatenate([jnp.concatenate([w2, zero_wa], axis=2),
                           jnp.concatenate([zero_wa, a2], axis=2)], axis=1).astype(BF16)
    v1p = jnp.pad(v1, ((0, 0), (0, 0), (0, LANES - VRES_LORA))).astype(BF16)
    v2p = jnp.pad(v2, ((0, 0), (0, LANES - VRES_LORA), (0, 0))).astype(BF16)
    rw_p = jnp.pad(router_w, ((0, 0), (0, 0), (0, LANES - N_EXPERTS)))
    rw_hi = rw_p.astype(BF16)
    rw_lo = (rw_p - rw_hi.astype(F32)).astype(BF16)
    rb_p = jnp.pad(router_b, ((0, 0), (0, LANES - N_EXPERTS)), constant_values=-1e30)
    b_gu4 = b_gu.reshape(depth, N_EXPERTS, 1, 2 * D_FF)
    b_dn4 = b_dn.reshape(depth, N_EXPERTS, 1, D_MODEL)

    def row2(a):
        return a.reshape(1, -1)

    x_p = x_prompt.reshape(n_p, D_MODEL)
    x_s = jnp.transpose(x_sample, (1, 0, 2)).reshape(n_s, D_MODEL)
    zeros_wkv = jnp.zeros((bp, RWKV_HEADS, HEAD_DIM, HEAD_DIM), F32)
    zeros_shift = jnp.zeros((bp, 1, SHIFT_DIM), F32)
    zeros_conv = jnp.zeros((bp, CONV_HIST, CONV_DIM), F32)
    wkv_s_t = jnp.transpose(state_wkv, (0, 2, 4, 3, 1))

    vf_p = vf_s = None
    wkv_p_out, wkv_s_out, shift_p_out, shift_s_out, conv_p_out, conv_s_out = [], [], [], [], [], []
    for l in range(depth):
        mod_p = mod_all[l, :bp].reshape(bp, 1, 6 * D_MODEL)
        mod_s = jnp.tile(mod_all[l, bp:], (ts, 1))
        wts = {"mu": row2(tm_mu[l]), "w0": row2(w0[l]), "a0": row2(a0[l]), "k_k": row2(k_k[l]),
               "k_a": row2(k_a[l]), "wwa": wwa[l], "g2": g2[l].astype(BF16),
               "lnx_g": row2(lnx_g[l]), "lnx_b": row2(lnx_b[l]), "r_k": row2(r_k[l]), "w_out": w_out_b[l],
               "ln1_g": row2(ln1_g[l]), "ln1_b": row2(ln1_b[l]), "router_w": rw_hi[l], "router_w_lo": rw_lo[l],
               "router_b": row2(rb_p[l])}
        if l > 0:
            wts.update({"v0": row2(v0[l - 1]), "v1": v1p[l - 1], "v2": v2p[l - 1]})

        per_group = []
        for grp, x, mod, is_p in ((gp, x_p, mod_p, True), (gs, x_s, mod_s, False)):
            if is_p:
                shift0, conv0 = zeros_shift, zeros_conv
            else:
                shift0 = state_shift[l][None]
                conv0 = jnp.transpose(state_conv[l], (1, 0, 2)).reshape(1, CONV_HIST * bs, CONV_DIM)
            vfirst = vf_p if is_p else vf_s
            r, lw, k, v, an, bn, g, sh_last, zc = _prep(grp, x, mod, w_in_b[l], shift0,
                                                        vfirst if l > 0 else None, wts, gmat)
            cv, tail = _conv(grp, zc, conv0, conv_w[l], row2(conv_b[l]), row2(conv_ln_g[l]), row2(conv_ln_b[l]))
            if l == 0:
                if is_p:
                    vf_p = v
                else:
                    vf_s = v
            if is_p:
                y, s_new = _wkv_chunked(bp, tp, r, lw, k, v, an, bn, zeros_wkv)
                wkv_p_out.append(s_new)
                shift_p_out.append(sh_last[:, 0])
            else:
                tmaj = lambda a: jnp.transpose(a.reshape(ts, bs, RWKV_DIM), (0, 2, 1))
                y_t, s_new = _wkv_lanes(ts, tmaj(r), tmaj(lw), tmaj(k), tmaj(v), tmaj(an), tmaj(bn), wkv_s_t[l])
                y = jnp.transpose(y_t, (0, 2, 1)).reshape(n_s, RWKV_DIM)
                wkv_s_out.append(jnp.transpose(s_new, (3, 0, 2, 1)))
                shift_s_out.append(sh_last[0])
            if is_p:
                conv_p_out.append(tail)
            else:
                conv_s_out.append(jnp.transpose(tail.reshape(CONV_HIST, bs, CONV_DIM), (1, 0, 2)))
            off = 0 if is_p else n_p
            x1, h2, ti, gt = _mix(grp, x, y, r, k, v, g, cv, mod, wts, gmat)
            per_group.append((grp, x1, h2, ti, gt, mod, off))

        top_i = jnp.concatenate([pg[3] for pg in per_group], axis=0)
        dest, block_e, n_used, n_rows = _route(top_i, gp.tm)
        if l == 0:
            x_buf = jnp.zeros((n_rows * LANE_TILES, LANES), F32)
        for grp, x1, h2, ti, gt, mod, off in per_group:
            x_buf = _dispatch(grp, h2, dest[off:off + grp.rows], x_buf)
        y_buf = _experts(l, x_buf, block_e, n_used, w_gu, b_gu4, w_dn, b_dn4)
        outs = []
        for grp, x1, h2, ti, gt, mod, off in per_group:
            outs.append(_combine(grp, x1, gt, dest[off:off + grp.rows], y_buf, mod, row2(ln2_g[l]), row2(ln2_b[l])))
        x_p, x_s = outs

    y_prompt = x_p.reshape(bp, tp, D_MODEL)
    y_sample = jnp.transpose(x_s.reshape(ts, bs, D_MODEL), (1, 0, 2))
    return (y_prompt, y_sample, jnp.stack(wkv_p_out), jnp.stack(wkv_s_out), jnp.stack(shift_p_out),
            jnp.stack(shift_s_out), jnp.stack(conv_p_out), jnp.stack(conv_s_out))
```

```python
import functools

import jax
import jax.numpy as jnp
from jax import lax
from jax.experimental import pallas as pl
from jax.experimental.pallas import tpu as pltpu

F32 = jnp.float32
BF16 = jnp.bfloat16

D_MODEL = 1024
DEPTH = 4
HEAD_DIM = 64
RWKV_DIM = 512
RWKV_HEADS = RWKV_DIM // HEAD_DIM
CONV_DIM = 512
CONV_WIDTH = 31
CONV_HIST = CONV_WIDTH - 1
DECAY_LORA = 64
AAA_LORA = 64
GATE_LORA = 128
VRES_LORA = 32
SHIFT_DIM = 3 * RWKV_DIM + DECAY_LORA + AAA_LORA + GATE_LORA
PROJ_DIM = SHIFT_DIM + 2 * CONV_DIM
N_EXPERTS = 32
TOP_K = 4
D_FF = D_MODEL
SWIGLU_LIMIT = 7.0
SWIGLU_ALPHA = 1.702
LN_EPS = 1e-5
GN_EPS = 64e-5
DEEPNORM_ALPHA = (2 * DEPTH) ** 0.25

LANES = 128
SUBLANES = 8
ROW_TILE = 512
CHUNK = 64
WKV_BATCHES_PER_STEP = 8
EXPERT_ROWS = 512
VMEM_LIMIT = 56 * 1024 * 1024


def _params(n_axes=1, vmem=VMEM_LIMIT):
    return pltpu.CompilerParams(dimension_semantics=("arbitrary",) * n_axes, vmem_limit_bytes=vmem)


def _round_up(x, m):
    return (x + m - 1) // m * m


def _bdot(a, b):
    return jnp.dot(a.astype(BF16), b.astype(BF16), preferred_element_type=F32)


def _split3(x):
    hi = x.astype(BF16)
    r1 = x - hi.astype(F32)
    mid = r1.astype(BF16)
    lo = (r1 - mid.astype(F32)).astype(BF16)
    return hi, mid, lo


def _headsum(x, g):
    return jnp.dot(x.astype(BF16), g, preferred_element_type=F32)


LANE_TILES = D_MODEL // LANES
assert LANE_TILES == SUBLANES


def _store_row_tiles(ref, x, rows, lead=()):
    for c in range(LANE_TILES):
        ref[lead + (pl.ds(c, rows, stride=LANE_TILES), slice(None))] = x[:, c * LANES:(c + 1) * LANES]


def _load_row_tiles(ref, rows, lead=()):
    return [ref[lead + (pl.ds(c, rows, stride=LANE_TILES), slice(None))] for c in range(LANE_TILES)]


def _row_tile(ref, r, lead=()):
    return ref.at[lead + (pl.ds(pl.multiple_of(r * LANE_TILES, LANE_TILES), LANE_TILES), slice(None))]


def _layernorm(x):
    mu = jnp.mean(x, axis=-1, keepdims=True)
    xc = x - mu
    var = jnp.mean(xc * xc, axis=-1, keepdims=True)
    return xc * lax.rsqrt(var + LN_EPS)


def _ada_body(c_ref, w_ref, b_ref, o_ref):
    c = c_ref[...]
    cs = c * jax.nn.sigmoid(c)
    o_ref[...] = _bdot(cs, w_ref[...]) + b_ref[...]


def _ada(c_all, ada_w, ada_b):
    depth, _, width = ada_w.shape
    nb = c_all.shape[0]
    tn = 1536
    return pl.pallas_call(
        _ada_body,
        grid=(depth, width // tn),
        in_specs=[pl.BlockSpec((nb, D_MODEL), lambda l, j: (0, 0)),
                  pl.BlockSpec((None, D_MODEL, tn), lambda l, j: (l, 0, j)),
                  pl.BlockSpec((None, 1, tn), lambda l, j: (l, 0, j))],
        out_specs=pl.BlockSpec((None, nb, tn), lambda l, j: (l, 0, j)),
        out_shape=jax.ShapeDtypeStruct((depth, nb, width), F32),
        compiler_params=_params(2),
        name="ada_mod",
    )(c_all, ada_w, ada_b.reshape(depth, 1, width))


class _Group:
    def __init__(self, rows, seg_len, stride, mod_per_row):
        self.rows = rows
        self.seg_len = seg_len
        self.stride = stride
        self.mod_per_row = mod_per_row
        self.tm = min(ROW_TILE, rows)
        assert rows % self.tm == 0 and seg_len % self.tm == 0
        self.tiles = rows // self.tm
        self.tiles_per_seg = seg_len // self.tm
        self.nseg = rows // seg_len


def _mod_spec(grp, which):
    if grp.mod_per_row:
        return pl.BlockSpec((grp.tm, D_MODEL), lambda i: (i, which))
    tps = grp.tiles_per_seg
    return pl.BlockSpec((None, 1, D_MODEL), lambda i: (i // tps, 0, which))


def _row_spec(grp, width):
    return pl.BlockSpec((grp.tm, width), lambda i: (i, 0))


def _const_spec(shape):
    nd = len(shape)
    return pl.BlockSpec(shape, lambda i: (0,) * nd)


def _prep_body(has_vres, grp, halo, *refs):
    it = iter(refs)
    x_ref, sc_ref, sh_ref, win_ref, sh0_ref = (next(it) for _ in range(5))
    vf_ref = next(it) if has_vres else None
    mu_ref, w0_ref, a0_ref, kk_ref, ka_ref, wwa_ref, g2_ref = (next(it) for _ in range(7))
    if has_vres:
        v0_ref, v1_ref, v2_ref = next(it), next(it), next(it)
    gmat_ref = next(it)
    r_o, lw_o, k_o, v_o, an_o, bn_o, g_o, shl_o, zc_o = (next(it) for _ in range(9))
    ext = next(it)

    tm, s = grp.tm, grp.stride
    j = pl.program_id(0) % grp.tiles_per_seg

    @pl.when(j == 0)
    def _():
        ext[pl.ds(halo - s, s), :] = sh0_ref[...]

    h = x_ref[...] * (1.0 + sc_ref[...]) + sh_ref[...]
    z = jnp.dot(h.astype(BF16), win_ref[...], preferred_element_type=F32)
    zc_o[...] = z[:, SHIFT_DIM:]
    zs = z[:, :SHIFT_DIM]
    ext[pl.ds(halo, tm), :] = zs
    prev = ext[pl.ds(halo - s, tm), :]
    shl_o[...] = ext[pl.ds(halo + tm - s, s), :]
    if grp.tiles_per_seg > 1:
        ext[pl.ds(0, halo), :] = ext[pl.ds(tm, halo), :]

    zm = zs + (prev - zs) * mu_ref[...]
    i1, i2, i3 = RWKV_DIM, 2 * RWKV_DIM, 3 * RWKV_DIM
    i5 = i3 + DECAY_LORA + AAA_LORA
    r, k, v = zm[:, :i1], zm[:, i1:i2], zm[:, i2:i3]
    zwa, zg = zm[:, i3:i5], zm[:, i5:]
    lane = lax.broadcasted_iota(jnp.int32, zwa.shape, 1)
    xwa = jnp.where(lane < DECAY_LORA, jnp.tanh(zwa), zwa)
    wa = _bdot(xwa, wwa_ref[...])
    w_in = w0_ref[...] + wa[:, :RWKV_DIM]
    neg = -w_in
    softplus = jnp.maximum(neg, 0.0) + jnp.log(1.0 + jnp.exp(-jnp.abs(neg)))
    w_log = -softplus - 0.5
    lw = -jnp.exp(w_log)
    a = jax.nn.sigmoid(a0_ref[...] + wa[:, RWKV_DIM:])
    g = _bdot(jax.nn.sigmoid(zg), g2_ref[...])
    if has_vres:
        vl = _bdot(_bdot(v, v1_ref[...]), v2_ref[...])
        vg = jax.nn.sigmoid(v0_ref[...] + vl)
        v = v + (vf_ref[...] - v) * vg
    kk = k * kk_ref[...]
    ss = _headsum(kk * kk, gmat_ref[...])
    kk = kk * lax.rsqrt(jnp.maximum(ss, 1e-24))
    k = k * (1.0 + (a - 1.0) * ka_ref[...])
    r_o[...] = r
    lw_o[...] = lw
    k_o[...] = k
    v_o[...] = v
    an_o[...] = -kk
    bn_o[...] = kk * a
    g_o[...] = g


def _prep(grp, x, mod, w_in_b, shift0, vfirst, wts, gmat):
    has_vres = vfirst is not None
    halo = _round_up(grp.stride, SUBLANES)
    s = grp.stride
    tps = grp.tiles_per_seg
    seg_spec = pl.BlockSpec((None, s, SHIFT_DIM), lambda i: (i // tps, 0, 0))
    in_specs = [_row_spec(grp, D_MODEL), _mod_spec(grp, 1), _mod_spec(grp, 0),
                _const_spec((D_MODEL, PROJ_DIM)), seg_spec]
    args = [x, mod, mod, w_in_b, shift0]
    if has_vres:
        in_specs.append(_row_spec(grp, RWKV_DIM))
        args.append(vfirst)
    names = ["mu", "w0", "a0", "k_k", "k_a", "wwa", "g2"] + (["v0", "v1", "v2"] if has_vres else [])
    for nm in names:
        in_specs.append(_const_spec(wts[nm].shape))
        args.append(wts[nm])
    in_specs.append(_const_spec(gmat.shape))
    args.append(gmat)
    out = jax.ShapeDtypeStruct((grp.rows, RWKV_DIM), F32)
    return pl.pallas_call(
        functools.partial(_prep_body, has_vres, grp, halo),
        grid=(grp.tiles,),
        in_specs=in_specs,
        out_specs=[_row_spec(grp, RWKV_DIM)] * 7 + [seg_spec, _row_spec(grp, 2 * CONV_DIM)],
        out_shape=[out] * 7 + [jax.ShapeDtypeStruct((grp.nseg, s, SHIFT_DIM), F32),
                               jax.ShapeDtypeStruct((grp.rows, 2 * CONV_DIM), F32)],
        scratch_shapes=[pltpu.VMEM((halo + grp.tm, SHIFT_DIM), F32)],
        compiler_params=_params(1),
        name="inproj_prep",
    )(*args)


def _nt(a, b):
    return lax.dot_general(a.astype(BF16), b.astype(BF16), (((1,), (1,)), ((), ())),
                           preferred_element_type=F32)


def _tn(a, b):
    return lax.dot_general(a.astype(BF16), b.astype(BF16), (((0,), (0,)), ((), ())),
                           preferred_element_type=F32)


def _wkv_chunk_body(nchunks, nb, r_ref, lw_ref, k_ref, v_ref, a_ref, b_ref, s0_ref, y_ref, so_ref, st):
    c = pl.program_id(1)
    L, N = CHUNK, HEAD_DIM

    @pl.when(c == 0)
    def _():
        st[...] = s0_ref[...]

    P2, W2 = 2 * L, 2 * N
    assert L == N and W2 == LANES
    trow = lax.broadcasted_iota(jnp.int32, (L, L), 0)
    tcol = lax.broadcasted_iota(jnp.int32, (L, L), 1)
    tri = jnp.where(trow >= tcol, 1.0, 0.0).astype(BF16)
    row = lax.broadcasted_iota(jnp.int32, (P2, P2), 0)
    col = lax.broadcasted_iota(jnp.int32, (P2, P2), 1)
    lg = L.bit_length() - 1
    same_head = (row >> lg) == (col >> lg)
    strict = same_head & (row > col)
    incl = same_head & (row >= col)
    eye = row == col
    sh = 3
    diag_blk = (row >> sh) == (col >> sh)
    off_masks = []
    while (1 << sh) < L:
        off_masks.append(((row >> sh) == (col >> sh) + 1) & ((row >> (sh + 1)) == (col >> (sh + 1))))
        sh += 1

    lane_lo = lax.broadcasted_iota(jnp.int32, (L, W2), 1) < N
    npair = RWKV_HEADS // 2

    def stack(x, p):
        xp = x[:, p * W2:(p + 1) * W2]
        return jnp.concatenate([jnp.where(lane_lo, xp, 0.0), jnp.where(lane_lo, 0.0, xp)], axis=0)

    at, rt, bt, kt, bh, kh, vv, p_last = [], [], [], [], [], [], [], []
    for bi in range(nb):
        lw = lw_ref[bi]
        h3 = _split3(lw)
        cl = (jnp.dot(tri, h3[0], preferred_element_type=F32) + jnp.dot(tri, h3[1], preferred_element_type=F32)
              + jnp.dot(tri, h3[2], preferred_element_type=F32))
        cl_last = cl[L - 1:L, :]
        ecl = jnp.exp(cl)
        encl = jnp.exp(-cl)
        dl = jnp.exp(cl_last - cl)
        pl_row = jnp.exp(cl_last)
        an, bn, kx, rx, vx = a_ref[bi], b_ref[bi], k_ref[bi], r_ref[bi], v_ref[bi]
        at_all, rt_all = an * jnp.exp(cl - lw), rx * ecl
        bt_all, kt_all, bh_all, kh_all = bn * encl, kx * encl, bn * dl, kx * dl
        for p in range(npair):
            at.append(stack(at_all, p).astype(BF16))
            rt.append(stack(rt_all, p))
            bt.append(stack(bt_all, p).astype(BF16))
            kt.append(stack(kt_all, p).astype(BF16))
            bh.append(stack(bh_all, p).astype(BF16))
            kh.append(stack(kh_all, p).astype(BF16))
            vv.append(stack(vx, p).astype(BF16))
            p_last.append(pl_row[:, p * W2:(p + 1) * W2])
    pairs = range(nb * npair)

    ar = [jnp.concatenate([at[p], rt[p].astype(BF16)], axis=0) for p in pairs]
    xb = [_nt(ar[p], bt[p]) for p in pairs]
    xk = [_nt(ar[p], kt[p]) for p in pairs]
    a_ab = [jnp.where(strict, xb[p][:P2], 0.0) for p in pairs]
    a_ak = [jnp.where(strict, xk[p][:P2], 0.0) for p in pairs]
    a_rb = [jnp.where(incl, xb[p][P2:], 0.0).astype(BF16) for p in pairs]
    a_rk = [jnp.where(incl, xk[p][P2:], 0.0) for p in pairs]
    pw = [jnp.where(diag_blk, a_ab[p], 0.0) for p in pairs]
    tm = [jnp.where(eye, 1.0, 0.0) + pw[p] for p in pairs]
    for _ in range(2):
        pw = [_bdot(pw[p], pw[p]) for p in pairs]
        tm = [tm[p] + _bdot(tm[p], pw[p]) for p in pairs]
    for off in off_masks:
        t1 = [_bdot(jnp.where(off, a_ab[p], 0.0), tm[p]) for p in pairs]
        tm = [tm[p] + _bdot(tm[p], t1[p]) for p in pairs]
    akv = [_bdot(a_ak[p], vv[p]).astype(BF16) for p in pairs]
    wu = [_bdot(tm[p], jnp.concatenate([at[p], akv[p]], axis=1)) for p in pairs]
    qy = [jnp.dot(a_rb[p], wu[p].astype(BF16), preferred_element_type=F32) for p in pairs]
    rkv = [_bdot(a_rk[p], vv[p]) for p in pairs]
    ys = []
    for p in pairs:
        bi, pp = divmod(p, npair)
        s_p = st[bi, pp]
        w, u0 = wu[p][:, :W2], wu[p][:, W2:]
        q = rt[p] + qy[p][:, :W2]
        ybd = _nt(q, s_p) + qy[p][:, W2:] + rkv[p]
        ys.append(ybd[:L] + ybd[L:])
        cm = _tn(u0, bh[p]) + _tn(vv[p], kh[p])
        st[bi, pp] = s_p * p_last[p] + _bdot(s_p, _tn(w, bh[p])) + cm
    for bi in range(nb):
        y_ref[bi] = jnp.concatenate(ys[bi * npair:(bi + 1) * npair], axis=1)

    @pl.when(c == nchunks - 1)
    def _():
        so_ref[...] = st[...]


def _wkv_chunked(nbatch, tlen, r, lw, k, v, an, bn, s0):
    nchunks = tlen // CHUNK
    npair, w2 = RWKV_HEADS // 2, 2 * HEAD_DIM
    s5 = s0.reshape(nbatch, npair, 2, HEAD_DIM, HEAD_DIM)
    zero = jnp.zeros_like(s5[:, :, 0])
    s_bd = jnp.concatenate([jnp.concatenate([s5[:, :, 0], zero], axis=-1),
                            jnp.concatenate([zero, s5[:, :, 1]], axis=-1)], axis=-2)
    nb = WKV_BATCHES_PER_STEP if nbatch % WKV_BATCHES_PER_STEP == 0 else 1
    tok = pl.BlockSpec((nb, CHUNK, RWKV_DIM), lambda b, c: (b, c, 0))
    stt = pl.BlockSpec((nb, npair, w2, w2), lambda b, c: (b, 0, 0, 0))
    seq = lambda t: t.reshape(nbatch, tlen, RWKV_DIM)
    y, so = pl.pallas_call(
        functools.partial(_wkv_chunk_body, nchunks, nb),
        grid=(nbatch // nb, nchunks),
        in_specs=[tok] * 6 + [stt],
        out_specs=[tok, stt],
        out_shape=[jax.ShapeDtypeStruct((nbatch, tlen, RWKV_DIM), F32),
                   jax.ShapeDtypeStruct((nbatch, npair, w2, w2), F32)],
        scratch_shapes=[pltpu.VMEM((nb, npair, w2, w2), F32)],
        compiler_params=_params(2),
        name="wkv_chunked",
    )(seq(r), seq(lw), seq(k), seq(v), seq(an), seq(bn), s_bd)
    s_new = jnp.stack([so[:, :, :HEAD_DIM, :HEAD_DIM], so[:, :, HEAD_DIM:, HEAD_DIM:]], axis=2)
    return y.reshape(nbatch * tlen, RWKV_DIM), s_new.reshape(nbatch, RWKV_HEADS, HEAD_DIM, HEAD_DIM)


def _wkv_lane_body(tlen, r_ref, lw_ref, k_ref, v_ref, a_ref, b_ref, s_ref, y_ref, so_ref):
    N = HEAD_DIM
    nb = s_ref.shape[-1]

    def row(ref, t, j):
        return jnp.broadcast_to(ref[t, pl.ds(j, 1), :], (N, nb))

    def first(j, acc):
        return acc + s_ref[j] * row(a_ref, 0, j)

    sa = lax.fori_loop(0, N, first, jnp.zeros((N, nb), F32))
    for t in range(tlen):
        v_t = v_ref[t]
        src = s_ref if t == 0 else so_ref

        def body(j, carry, t=t, v_t=v_t, src=src, sa=sa):
            y, sa_next = carry
            sj = src[j] * jnp.exp(row(lw_ref, t, j)) + sa * row(b_ref, t, j) + v_t * row(k_ref, t, j)
            so_ref[j] = sj
            y = y + sj * row(r_ref, t, j)
            if t + 1 < tlen:
                sa_next = sa_next + sj * row(a_ref, t + 1, j)
            return y, sa_next

        zero = jnp.zeros((N, nb), F32)
        y, sa = lax.fori_loop(0, N, body, (zero, zero))
        y_ref[t] = y


def _wkv_lanes(tlen, r, lw, k, v, an, bn, s_t):
    nb = r.shape[-1]
    tok = pl.BlockSpec((tlen, HEAD_DIM, nb), lambda h: (0, h, 0))
    stt = pl.BlockSpec((None, HEAD_DIM, HEAD_DIM, nb), lambda h: (h, 0, 0, 0))
    return pl.pallas_call(
        functools.partial(_wkv_lane_body, tlen),
        grid=(RWKV_HEADS,),
        in_specs=[tok] * 6 + [stt],
        out_specs=[tok, stt],
        out_shape=[jax.ShapeDtypeStruct((tlen, RWKV_DIM, nb), F32),
                   jax.ShapeDtypeStruct((RWKV_HEADS, HEAD_DIM, HEAD_DIM, nb), F32)],
        compiler_params=_params(1),
        name="wkv_lanes",
    )(r, lw, k, v, an, bn, s_t)


def _conv_body(grp, halo, zc_ref, c0_ref, cw_ref, cb_ref, lg_ref, lb_ref, cv_ref, tail_ref, ext, win):
    tm, s = grp.tm, grp.stride
    hist = CONV_HIST * s
    j = pl.program_id(0) % grp.tiles_per_seg

    @pl.when(j == 0)
    def _():
        ext[pl.ds(halo - hist, hist), :] = c0_ref[...]

    zc = zc_ref[...]
    u = zc[:, :CONV_DIM] * jax.nn.sigmoid(zc[:, CONV_DIM:])
    ext[pl.ds(halo, tm), :] = u
    acc = jnp.broadcast_to(cb_ref[...], (tm, CONV_DIM))
    offs = [halo - hist + w * s for w in range(CONV_WIDTH)]
    for phase in range(SUBLANES):
        taps = [w for w in range(CONV_WIDTH) if offs[w] % SUBLANES == phase]
        if not taps:
            continue
        if s % SUBLANES == 0:
            for w in taps:
                acc = acc + cw_ref[pl.ds(w, 1), :] * ext[pl.ds(offs[w], tm), :]
            continue
        lo = offs[taps[0]]
        span = tm + offs[taps[-1]] - lo
        win[pl.ds(0, span), :] = ext[pl.ds(lo, span), :]
        for w in taps:
            acc = acc + cw_ref[pl.ds(w, 1), :] * win[pl.ds(offs[w] - lo, tm), :]
    y = _layernorm(acc) * lg_ref[...] + lb_ref[...]
    cv_ref[...] = y * jax.nn.sigmoid(y)
    tail_ref[...] = ext[pl.ds(halo + tm - hist, hist), :]
    if grp.tiles_per_seg > 1:
        ext[pl.ds(0, halo), :] = ext[pl.ds(tm, halo), :]


def _conv(grp, zc, conv0, cw, cb, lg, lb):
    hist = CONV_HIST * grp.stride
    halo = _round_up(hist, SUBLANES)
    tps = grp.tiles_per_seg
    hist_spec = pl.BlockSpec((None, hist, CONV_DIM), lambda i: (i // tps, 0, 0))
    return pl.pallas_call(
        functools.partial(_conv_body, grp, halo),
        grid=(grp.tiles,),
        in_specs=[_row_spec(grp, 2 * CONV_DIM), hist_spec,
                  _const_spec(cw.shape), _const_spec(cb.shape), _const_spec(lg.shape), _const_spec(lb.shape)],
        out_specs=[_row_spec(grp, CONV_DIM), hist_spec],
        out_shape=[jax.ShapeDtypeStruct((grp.rows, CONV_DIM), F32),
                   jax.ShapeDtypeStruct((grp.nseg, hist, CONV_DIM), F32)],
        scratch_shapes=[pltpu.VMEM((halo + grp.tm, CONV_DIM), F32),
                        pltpu.VMEM((grp.tm + _round_up(CONV_HIST, SUBLANES), CONV_DIM), F32)],
        compiler_params=_params(1),
        name="conv_group",
    )(zc, conv0, cw, cb, lg, lb)


def _mix_body(x_ref, y_ref, r_ref, k_ref, v_ref, g_ref, cv_ref, g1_ref, sc2_ref, sh2_ref,
              lnxg_ref, lnxb_ref, rk_ref, wout_ref, l1g_ref, l1b_ref, rw_ref, rwl_ref, rb_ref, gmat_ref,
              x1_ref, h2_ref, ti_ref, gt_ref, cnt_ref):
    gm = gmat_ref[...]
    y = y_ref[...]
    inv = 1.0 / HEAD_DIM
    ym = _headsum(y, gm) * inv
    yc = y - ym
    yv = _headsum(yc * yc, gm) * inv
    yn = yc * lax.rsqrt(yv + GN_EPS) * lnxg_ref[...] + lnxb_ref[...]
    bonus = _headsum(r_ref[...] * k_ref[...] * rk_ref[...], gm) * v_ref[...]
    yo = (yn + bonus) * g_ref[...]
    mix = (jnp.dot(yo.astype(BF16), wout_ref[pl.ds(0, RWKV_DIM), :], preferred_element_type=F32)
           + jnp.dot(cv_ref[...].astype(BF16), wout_ref[pl.ds(RWKV_DIM, CONV_DIM), :],
                     preferred_element_type=F32))
    xa = DEEPNORM_ALPHA * x_ref[...] + g1_ref[...] * mix
    x1 = _layernorm(xa) * l1g_ref[...] + l1b_ref[...]
    x1_ref[...] = x1
    h2 = x1 * (1.0 + sc2_ref[...]) + sh2_ref[...]
    _store_row_tiles(h2_ref, h2, h2.shape[0])
    h_hi = h2.astype(BF16)
    h_lo = (h2 - h_hi.astype(F32)).astype(BF16)
    logits = (jnp.dot(h_hi, rw_ref[...], preferred_element_type=F32)
              + jnp.dot(h_hi, rwl_ref[...], preferred_element_type=F32)
              + jnp.dot(h_lo, rw_ref[...], preferred_element_type=F32)) + rb_ref[...]
    lane = lax.broadcasted_iota(jnp.int32, logits.shape, 1).astype(F32)
    cur = logits
    vals, idxs = [], []
    for _ in range(TOP_K):
        m = jnp.max(cur, axis=-1, keepdims=True)
        idx = jnp.min(jnp.where(cur == m, lane, float(LANES)), axis=-1, keepdims=True)
        vals.append(m)
        idxs.append(idx)
        cur = jnp.where(lane == idx, -jnp.inf, cur)
    es = [jnp.exp(vv - vals[0]) for vv in vals]
    den = es[0] + es[1] + es[2] + es[3]
    ti = jnp.zeros_like(logits)
    gt = jnp.zeros_like(logits)
    for kk in range(TOP_K):
        ti = jnp.where(lane == float(kk), idxs[kk], ti)
        gt = jnp.where(lane == float(kk), es[kk] / den, gt)
    ti_ref[...] = ti.astype(jnp.int32)
    gt_ref[...] = gt
    picked = jnp.zeros_like(logits)
    for kk in range(TOP_K):
        picked = picked + jnp.where(lane == idxs[kk], 1.0, 0.0)
    cnt_ref[...] = jnp.sum(picked, axis=0, keepdims=True)


def _mix(grp, x, y, r, k, v, g, cv, mod, wts, gmat):
    half = _row_spec(grp, RWKV_DIM)
    names = ["lnx_g", "lnx_b", "r_k", "w_out", "ln1_g", "ln1_b", "router_w", "router_w_lo", "router_b"]
    in_specs = ([_row_spec(grp, D_MODEL)] + [half] * 6 + [_mod_spec(grp, 2), _mod_spec(grp, 4), _mod_spec(grp, 3)]
                + [_const_spec(wts[nm].shape) for nm in names] + [_const_spec(gmat.shape)])
    args = [x, y, r, k, v, g, cv, mod, mod, mod] + [wts[nm] for nm in names] + [gmat]
    return pl.pallas_call(
        _mix_body,
        grid=(grp.tiles,),
        in_specs=in_specs,
        out_specs=[_row_spec(grp, D_MODEL), pl.BlockSpec((grp.tm * LANE_TILES, LANES), lambda i: (i, 0)),
                   _row_spec(grp, LANES), _row_spec(grp, LANES),
                   pl.BlockSpec((None, 1, LANES), lambda i: (i, 0, 0))],
        out_shape=[jax.ShapeDtypeStruct((grp.rows, D_MODEL), F32),
                   jax.ShapeDtypeStruct((grp.rows * LANE_TILES, LANES), F32),
                   jax.ShapeDtypeStruct((grp.rows, LANES), jnp.int32),
                   jax.ShapeDtypeStruct((grp.rows, LANES), F32),
                   jax.ShapeDtypeStruct((grp.tiles, 1, LANES), F32)],
        compiler_params=_params(1),
        name="mix_out",
    )(*args)


def _route_body(ti_ref, cnt_ref, dest_ref, carry, pstart):
    i = pl.program_id(0)
    ti = ti_ref[...]
    tm = ti.shape[0]
    lane = lax.broadcasted_iota(jnp.int32, (tm, LANES), 1)
    sel = [lane == ti[:, kk:kk + 1] for kk in range(TOP_K)]
    onehot = jnp.zeros((tm, LANES), F32)
    for kk in range(TOP_K):
        onehot = onehot + jnp.where(sel[kk], 1.0, 0.0)

    @pl.when(i == 0)
    def _():
        cnt = cnt_ref[...]
        padded = jnp.floor((cnt + (EXPERT_ROWS - 1)) * (1.0 / EXPERT_ROWS)) * EXPERT_ROWS
        er = lax.broadcasted_iota(jnp.int32, (LANES, LANES), 0)
        ec = lax.broadcasted_iota(jnp.int32, (LANES, LANES), 1)
        earlier = jnp.where(er < ec, 1.0, 0.0).astype(BF16)
        h3 = _split3(jnp.broadcast_to(padded, (SUBLANES, LANES)))
        acc = (jnp.dot(h3[0], earlier, preferred_element_type=F32)
               + jnp.dot(h3[1], earlier, preferred_element_type=F32)
               + jnp.dot(h3[2], earlier, preferred_element_type=F32))
        pstart[...] = acc[0:1, :]
        carry[...] = jnp.zeros(carry.shape, F32)

    rr = lax.broadcasted_iota(jnp.int32, (tm, tm), 0)
    cc = lax.broadcasted_iota(jnp.int32, (tm, tm), 1)
    before = jnp.where(rr > cc, 1.0, 0.0).astype(BF16)
    cum = jnp.dot(before, onehot.astype(BF16), preferred_element_type=F32) + (carry[...] + pstart[...])
    dest = jnp.zeros((tm, LANES), F32)
    for kk in range(TOP_K):
        pk = jnp.sum(jnp.where(sel[kk], cum, 0.0), axis=-1, keepdims=True)
        dest = jnp.where(lane == kk, pk, dest)
    dest_ref[...] = dest.astype(jnp.int32)
    carry[...] = carry[...] + jnp.sum(onehot, axis=0, keepdims=True)


def _route(top_i, cnt, tm):
    n_tok = top_i.shape[0]
    rows = EXPERT_ROWS
    dest = pl.pallas_call(
        _route_body,
        grid=(n_tok // tm,),
        in_specs=[pl.BlockSpec((tm, LANES), lambda i: (i, 0)), _const_spec((1, LANES))],
        out_specs=pl.BlockSpec((tm, LANES), lambda i: (i, 0)),
        out_shape=jax.ShapeDtypeStruct((n_tok, LANES), jnp.int32),
        scratch_shapes=[pltpu.VMEM((1, LANES), F32), pltpu.VMEM((1, LANES), F32)],
        compiler_params=_params(1),
        name="route_rank",
    )(top_i, cnt)
    n_blocks = -(-(n_tok * TOP_K) // rows) + N_EXPERTS
    counts = cnt[0, :N_EXPERTS].astype(jnp.int32)
    padded_end = jnp.cumsum((counts + rows - 1) // rows * rows)
    block_start = jnp.arange(n_blocks, dtype=jnp.int32) * rows
    block_e = jnp.sum((padded_end[None, :] <= block_start[:, None]).astype(jnp.int32), axis=1)
    block_e = jnp.minimum(block_e, N_EXPERTS - 1)
    n_used = (padded_end[-1] // rows).astype(jnp.int32).reshape(1)
    return dest[:, :TOP_K], block_e, n_used, n_blocks * rows


def _dispatch_body(tm, dst_ref, h_ref, xin_ref, x_hbm, sem):
    del xin_ref

    def body(r, carry):
        for kk in range(TOP_K):
            d = dst_ref[0, r * TOP_K + kk]
            pltpu.make_async_copy(_row_tile(h_ref, r), _row_tile(x_hbm, d), sem.at[kk % 2]).start(priority=kk % 2)
        return carry

    lax.fori_loop(0, tm, body, 0, unroll=8)
    for kk in range(TOP_K):
        pltpu.make_async_copy(h_ref, h_ref, sem.at[kk % 2]).wait()


def _dispatch(grp, h2, dest, x_buf):
    tm = grp.tm
    dblk = dest.reshape(grp.tiles, 1, tm * TOP_K)
    return pl.pallas_call(
        functools.partial(_dispatch_body, tm),
        grid=(grp.tiles,),
        in_specs=[pl.BlockSpec((None, 1, tm * TOP_K), lambda i: (i, 0, 0), memory_space=pltpu.SMEM),
                  pl.BlockSpec((tm * LANE_TILES, LANES), lambda i: (i, 0)), pl.BlockSpec(memory_space=pl.ANY)],
        out_specs=pl.BlockSpec(memory_space=pl.ANY),
        out_shape=jax.ShapeDtypeStruct(x_buf.shape, F32),
        scratch_shapes=[pltpu.SemaphoreType.DMA((2,))],
        input_output_aliases={2: 0},
        compiler_params=pltpu.CompilerParams(dimension_semantics=("arbitrary",), vmem_limit_bytes=VMEM_LIMIT,
                                             has_side_effects=True),
        name="dispatch",
    )(dblk, h2, x_buf)


def _expert_body(be_ref, nu_ref, gu_ref, dn_ref, x_ref, wgu_ref, bgu_ref, wdn_ref, bdn_ref, o_ref, wgu_b, wdn_b):
    del gu_ref, dn_ref
    i = pl.program_id(0)

    @pl.when(i < nu_ref[0])
    def _():
        prev_e = be_ref[jnp.maximum(i - 1, 0)]

        @pl.when((i == 0) | (be_ref[i] != prev_e))
        def _():
            wgu_b[...] = wgu_ref[...].astype(BF16)
            wdn_b[...] = wdn_ref[...].astype(BF16)

        x = jnp.concatenate(_load_row_tiles(x_ref, EXPERT_ROWS), axis=1).astype(BF16)
        gu = jnp.dot(x, wgu_b[...], preferred_element_type=F32) + bgu_ref[...]
        gate = jnp.minimum(gu[:, :D_FF], SWIGLU_LIMIT)
        lin = jnp.clip(gu[:, D_FF:], -SWIGLU_LIMIT, SWIGLU_LIMIT)
        act = gate * jax.nn.sigmoid(SWIGLU_ALPHA * gate) * (lin + 1.0)
        out = jnp.dot(act.astype(BF16), wdn_b[...], preferred_element_type=F32) + bdn_ref[...]
        _store_row_tiles(o_ref, out, EXPERT_ROWS)

    @pl.when(i >= nu_ref[0])
    def _():
        o_ref[...] = jnp.zeros(o_ref.shape, F32)


def _experts(layer, x_buf, block_e, n_used, w_gu, b_gu, w_dn, b_dn):
    rows = EXPERT_ROWS
    n_blocks = x_buf.shape[0] // (rows * LANE_TILES)
    xblk = (rows * LANE_TILES, LANES)

    idx = jnp.arange(n_blocks, dtype=jnp.int32)
    be = block_e[jnp.minimum(idx, n_used[0] - 1)]
    first = jnp.concatenate([jnp.ones((1,), bool), be[1:] != be[:-1]])
    run_start = lax.cummax(jnp.where(first, idx, 0), axis=0)
    first_at_or_after = jnp.flip(lax.cummin(jnp.flip(jnp.where(first, idx, n_blocks)), axis=0))
    next_first = jnp.concatenate([first_at_or_after[1:], jnp.full((1,), n_blocks, jnp.int32)])
    next_e = jnp.where(next_first < n_blocks, be[jnp.minimum(next_first, n_blocks - 1)], be)
    gu_idx = jnp.where(first, be, next_e)
    dn_idx = jnp.where(first | (idx == run_start + 1), be, next_e)

    grid_spec = pltpu.PrefetchScalarGridSpec(
        num_scalar_prefetch=4,
        grid=(n_blocks,),
        in_specs=[pl.BlockSpec(xblk, lambda i, be, nu, gu, dn: (jnp.minimum(i, nu[0] - 1), 0)),
                  pl.BlockSpec((None, None, D_MODEL, 2 * D_FF), lambda i, be, nu, gu, dn: (layer, gu[i], 0, 0)),
                  pl.BlockSpec((None, None, 1, 2 * D_FF), lambda i, be, nu, gu, dn: (layer, be[i], 0, 0)),
                  pl.BlockSpec((None, None, D_FF, D_MODEL), lambda i, be, nu, gu, dn: (layer, dn[i], 0, 0)),
                  pl.BlockSpec((None, None, 1, D_MODEL), lambda i, be, nu, gu, dn: (layer, be[i], 0, 0))],
        out_specs=pl.BlockSpec(xblk, lambda i, be, nu, gu, dn: (i, 0)),
        scratch_shapes=[pltpu.VMEM((D_MODEL, 2 * D_FF), BF16), pltpu.VMEM((D_FF, D_MODEL), BF16)])
    return pl.pallas_call(
        _expert_body,
        grid_spec=grid_spec,
        out_shape=jax.ShapeDtypeStruct(x_buf.shape, F32),
        compiler_params=_params(1),
        name="experts",
    )(be, n_used, gu_idx, dn_idx, x_buf, w_gu, b_gu, w_dn, b_dn)


def _combine_body(tm, dcur_ref, dnxt_ref, x1_ref, gt_ref, g2_ref, lg_ref, lb_ref, y_hbm, o_ref, ybuf, sem):
    i = pl.program_id(0)
    slot = i % 2

    def gather(dref, sl):
        def body(r, carry):
            for kk in range(TOP_K):
                d = dref[0, r * TOP_K + kk]
                pltpu.make_async_copy(_row_tile(y_hbm, d), _row_tile(ybuf, r, (sl, kk)),
                                      sem.at[sl, kk % 2]).start(priority=kk % 2)
            return carry
        lax.fori_loop(0, tm, body, 0, unroll=8)

    @pl.when(i == 0)
    def _():
        gather(dcur_ref, 0)

    @pl.when(i + 1 < pl.num_programs(0))
    def _():
        gather(dnxt_ref, 1 - slot)

    for kk in range(TOP_K):
        pltpu.make_async_copy(ybuf.at[slot, kk], ybuf.at[slot, kk], sem.at[slot, kk % 2]).wait()
    gt = gt_ref[...]
    parts = None
    for kk in range(TOP_K):
        gk = jnp.broadcast_to(gt[:, kk:kk + 1], (tm, LANES))
        yk = [gk * t for t in _load_row_tiles(ybuf, tm, (slot, kk))]
        parts = yk if parts is None else [a + b for a, b in zip(parts, yk)]
    ff = jnp.concatenate(parts, axis=1)
    xa = DEEPNORM_ALPHA * x1_ref[...] + g2_ref[...] * ff
    o_ref[...] = _layernorm(xa) * lg_ref[...] + lb_ref[...]


def _combine(grp, x1, gates, dest, y_buf, mod, lg, lb):
    tm = grp.tm
    last = grp.tiles - 1
    dblk = dest.reshape(grp.tiles, 1, tm * TOP_K)
    dspec = lambda f: pl.BlockSpec((None, 1, tm * TOP_K), f, memory_space=pltpu.SMEM)
    return pl.pallas_call(
        functools.partial(_combine_body, tm),
        grid=(grp.tiles,),
        in_specs=[dspec(lambda i: (i, 0, 0)), dspec(lambda i: (jnp.minimum(i + 1, last), 0, 0)),
                  _row_spec(grp, D_MODEL), _row_spec(grp, LANES), _mod_spec(grp, 5),
                  _const_spec(lg.shape), _const_spec(lb.shape), pl.BlockSpec(memory_space=pl.ANY)],
        out_specs=_row_spec(grp, D_MODEL),
        out_shape=jax.ShapeDtypeStruct((grp.rows, D_MODEL), F32),
        scratch_shapes=[pltpu.VMEM((2, TOP_K, tm * LANE_TILES, LANES), F32), pltpu.SemaphoreType.DMA((2, 2))],
        compiler_params=_params(1),
        name="combine_ln",
    )(dblk, dblk, x1, gates, mod, lg, lb, y_buf)


def kernel(x_prompt, x_sample, state_wkv, state_shift, state_conv, c_prompt, c_sample, w_in, tm_mu, w0, w2, a0, a2, g2, v0, v1, v2, k_k, k_a, r_k, lnx_g, lnx_b, conv_w, conv_b, conv_ln_g, conv_ln_b, w_out, ada_w, ada_b, ln1_g, ln1_b, ln2_g, ln2_b, router_w, router_b, w_gu, b_gu, w_dn, b_dn):
    bp, tp, _ = x_prompt.shape
    bs, ts, _ = x_sample.shape
    depth = w_in.shape[0]
    n_p, n_s = bp * tp, bs * ts
    n_tok = n_p + n_s
    gp = _Group(n_p, tp, 1, False)
    gs = _Group(n_s, n_s, bs, True)

    mod_all = _ada(jnp.concatenate([c_prompt, c_sample], axis=0), ada_w, ada_b)

    w_in_b = w_in.astype(BF16)
    w_out_b = w_out.astype(BF16)
    hid = jnp.arange(RWKV_DIM) // HEAD_DIM
    gmat = (hid[:, None] == hid[None, :]).astype(BF16)
    zero_wa = jnp.zeros((depth, DECAY_LORA, RWKV_DIM), F32)
    wwa = jnp.concatenate([jnp.concatenate([w2, zero_wa], axis=2),
                           jnp.concatenate([zero_wa, a2], axis=2)], axis=1).astype(BF16)
    v1p = jnp.pad(v1, ((0, 0), (0, 0), (0, LANES - VRES_LORA))).astype(BF16)
    v2p = jnp.pad(v2, ((0, 0), (0, LANES - VRES_LORA), (0, 0))).astype(BF16)
    rw_p = jnp.pad(router_w, ((0, 0), (0, 0), (0, LANES - N_EXPERTS)))
    rw_hi = rw_p.astype(BF16)
    rw_lo = (rw_p - rw_hi.astype(F32)).astype(BF16)
    rb_p = jnp.pad(router_b, ((0, 0), (0, LANES - N_EXPERTS)), constant_values=-1e30)
    b_gu4 = b_gu.reshape(depth, N_EXPERTS, 1, 2 * D_FF)
    b_dn4 = b_dn.reshape(depth, N_EXPERTS, 1, D_MODEL)

    def row2(a):
        return a.reshape(1, -1)

    x_p = x_prompt.reshape(n_p, D_MODEL)
    x_s = jnp.transpose(x_sample, (1, 0, 2)).reshape(n_s, D_MODEL)
    zeros_wkv = jnp.zeros((bp, RWKV_HEADS, HEAD_DIM, HEAD_DIM), F32)
    zeros_shift = jnp.zeros((bp, 1, SHIFT_DIM), F32)
    zeros_conv = jnp.zeros((bp, CONV_HIST, CONV_DIM), F32)
    wkv_s_t = jnp.transpose(state_wkv, (0, 2, 4, 3, 1))

    vf_p = vf_s = None
    wkv_p_out, wkv_s_out, shift_p_out, shift_s_out, conv_p_out, conv_s_out = [], [], [], [], [], []
    for l in range(depth):
        mod_p = mod_all[l, :bp].reshape(bp, 1, 6 * D_MODEL)
        mod_s = jnp.tile(mod_all[l, bp:], (ts, 1))
        wts = {"mu": row2(tm_mu[l]), "w0": row2(w0[l]), "a0": row2(a0[l]), "k_k": row2(k_k[l]),
               "k_a": row2(k_a[l]), "wwa": wwa[l], "g2": g2[l].astype(BF16),
               "lnx_g": row2(lnx_g[l]), "lnx_b": row2(lnx_b[l]), "r_k": row2(r_k[l]), "w_out": w_out_b[l],
               "ln1_g": row2(ln1_g[l]), "ln1_b": row2(ln1_b[l]), "router_w": rw_hi[l], "router_w_lo": rw_lo[l],
               "router_b": row2(rb_p[l])}
        if l > 0:
            wts.update({"v0": row2(v0[l - 1]), "v1": v1p[l - 1], "v2": v2p[l - 1]})

        per_group = []
        for grp, x, mod, is_p in ((gp, x_p, mod_p, True), (gs, x_s, mod_s, False)):
            if is_p:
                shift0, conv0 = zeros_shift, zeros_conv
            else:
                shift0 = state_shift[l][None]
                conv0 = jnp.transpose(state_conv[l], (1, 0, 2)).reshape(1, CONV_HIST * bs, CONV_DIM)
            vfirst = vf_p if is_p else vf_s
            r, lw, k, v, an, bn, g, sh_last, zc = _prep(grp, x, mod, w_in_b[l], shift0,
                                                        vfirst if l > 0 else None, wts, gmat)
            cv, tail = _conv(grp, zc, conv0, conv_w[l], row2(conv_b[l]), row2(conv_ln_g[l]), row2(conv_ln_b[l]))
            if l == 0:
                if is_p:
                    vf_p = v
                else:
                    vf_s = v
            if is_p:
                y, s_new = _wkv_chunked(bp, tp, r, lw, k, v, an, bn, zeros_wkv)
                wkv_p_out.append(s_new)
                shift_p_out.append(sh_last[:, 0])
            else:
                tmaj = lambda a: jnp.transpose(a.reshape(ts, bs, RWKV_DIM), (0, 2, 1))
                y_t, s_new = _wkv_lanes(ts, tmaj(r), tmaj(lw), tmaj(k), tmaj(v), tmaj(an), tmaj(bn), wkv_s_t[l])
                y = jnp.transpose(y_t, (0, 2, 1)).reshape(n_s, RWKV_DIM)
                wkv_s_out.append(jnp.transpose(s_new, (3, 0, 2, 1)))
                shift_s_out.append(sh_last[0])
            if is_p:
                conv_p_out.append(tail)
            else:
                conv_s_out.append(jnp.transpose(tail.reshape(CONV_HIST, bs, CONV_DIM), (1, 0, 2)))
            off = 0 if is_p else n_p
            x1, h2, ti, gt, tile_cnt = _mix(grp, x, y, r, k, v, g, cv, mod, wts, gmat)
            per_group.append((grp, x1, h2, ti, gt, mod, off))
            pair_cnt = tile_cnt.sum(axis=0) if is_p else pair_cnt + tile_cnt.sum(axis=0)

        top_i = jnp.concatenate([pg[3] for pg in per_group], axis=0)
        dest, block_e, n_used, n_rows = _route(top_i, pair_cnt, gp.tm)
        if l == 0:
            x_buf = jnp.zeros((n_rows * LANE_TILES, LANES), F32)
        for grp, x1, h2, ti, gt, mod, off in per_group:
            x_buf = _dispatch(grp, h2, dest[off:off + grp.rows], x_buf)
        y_buf = _experts(l, x_buf, block_e, n_used, w_gu, b_gu4, w_dn, b_dn4)
        outs = []
        for grp, x1, h2, ti, gt, mod, off in per_group:
            outs.append(_combine(grp, x1, gt, dest[off:off + grp.rows], y_buf, mod, row2(ln2_g[l]), row2(ln2_b[l])))
        x_p, x_s = outs

    y_prompt = x_p.reshape(bp, tp, D_MODEL)
    y_sample = jnp.transpose(x_s.reshape(ts, bs, D_MODEL), (1, 0, 2))
    return (y_prompt, y_sample, jnp.stack(wkv_p_out), jnp.stack(wkv_s_out), jnp.stack(shift_p_out),
            jnp.stack(shift_s_out), jnp.stack(conv_p_out), jnp.stack(conv_s_out))
```
